```python
import jax, jax.numpy as jnp
from jax import lax
import numpy as np

D_MODEL = 4096
BATCH = 2
SEQ = 4096
DEPTH = 2

N_BRANCH = 4
BRANCH_WIDTH = D_MODEL // N_BRANCH
CONV_WIDTH = 31
CONV_PAD = CONV_WIDTH // 2
POOL_WINDOWS = (2, 4, 8, 16)
N_POOL_GROUPS = len(POOL_WINDOWS)
POOL_GROUP = BRANCH_WIDTH // N_POOL_GROUPS
N_FFT_GROUPS = 4
FFT_GROUP = BRANCH_WIDTH // N_FFT_GROUPS
GMLP_CHUNK = 128
GMLP_HEAD_DIM = 128
GMLP_HEADS = BRANCH_WIDTH // GMLP_HEAD_DIM
IN_COLS = 6 * BRANCH_WIDTH
D_FF = 256 * ((8 * D_MODEL // 3 + 255) // 256)
N_EXPERTS = 8
TOP_K = 2
D_FF_EXPERT = D_MODEL
EPS = 1e-6

kernel_name = 'hybrid_conv_pool_fourier_gmlp_moe_encoder'


def rmsnorm(x, g):
    xf = x.astype(jnp.float32)
    y = xf * lax.rsqrt(jnp.mean(xf * xf, axis=-1, keepdims=True) + EPS)
    return y.astype(x.dtype) * g


def layernorm(x, g, b):
    xf = x.astype(jnp.float32)
    mu = jnp.mean(xf, axis=-1, keepdims=True)
    var = jnp.mean(jnp.square(xf - mu), axis=-1, keepdims=True)
    return ((xf - mu) * lax.rsqrt(var + EPS)).astype(x.dtype) * g + b


def conv_branch(a_lin, a_gate, conv_w, conv_b, ln_g, ln_b):
    a = a_lin * jax.nn.sigmoid(a_gate)
    y = lax.conv_general_dilated(a, conv_w[:, None, :], (1,), [(CONV_PAD, CONV_PAD)],
                                 dimension_numbers=('NWC', 'WIO', 'NWC'),
                                 feature_group_count=BRANCH_WIDTH) + conv_b
    return jax.nn.silu(layernorm(y, ln_g, ln_b))


def pool_branch(p, pool_w, pool_scale):
    B, S, _ = p.shape
    pf = p.astype(jnp.float32).reshape(B, S, N_POOL_GROUPS, POOL_GROUP)
    cs = jnp.concatenate([jnp.zeros((B, 1, N_POOL_GROUPS, POOL_GROUP), jnp.float32),
                          jnp.cumsum(pf, axis=1)], axis=1)
    t = jnp.arange(S)
    outs = []
    for g, w in enumerate(POOL_WINDOWS):
        lo = jnp.clip(t - w // 2, 0, S - 1)
        hi = jnp.clip(t + w // 2 - 1, 0, S - 1)
        win_sum = cs[:, hi + 1, g] - cs[:, lo, g]
        cnt = (hi - lo + 1).astype(jnp.float32)[None, :, None]
        outs.append(win_sum / cnt - pf[:, :, g])
    pooled = jnp.stack(outs, axis=2).astype(p.dtype)
    mixed = jnp.einsum('bsgc,gce->bsge', pooled, pool_w)
    return mixed.reshape(B, S, BRANCH_WIDTH) * pool_scale


def fourier_branch(f):
    B, S, _ = f.shape
    fg = f.astype(jnp.float32).reshape(B, S, N_FFT_GROUPS, FFT_GROUP)
    y = jnp.fft.fft2(fg, axes=(1, 3), norm='ortho').real
    return y.astype(f.dtype).reshape(B, S, BRANCH_WIDTH)


def gmlp_branch(u, v, ln_g, ln_b, ws, bias):
    B, S, _ = v.shape
    vn = layernorm(v, ln_g, ln_b).reshape(B, S // GMLP_CHUNK, GMLP_CHUNK, GMLP_HEADS, GMLP_HEAD_DIM)
    s = jnp.einsum('hpq,bnqhc->bnphc', ws, vn) + bias.T[None, None, :, :, None]
    return u * s.reshape(B, S, BRANCH_WIDTH)


def hybrid_mixer(xn, w_in, w_gate, conv_w, conv_b, conv_ln_g, conv_ln_b, pool_w, pool_scale,
                 gmlp_ln_g, gmlp_ln_b, gmlp_ws, gmlp_b, w_branch, w_out):
    B, S, _ = xn.shape
    h = xn @ w_in
    a_lin, a_gate, p_in, f_in, u, v = jnp.split(h, 6, axis=-1)
    ya = conv_branch(a_lin, a_gate, conv_w, conv_b, conv_ln_g, conv_ln_b)
    yb = pool_branch(p_in, pool_w, pool_scale)
    yc = fourier_branch(f_in)
    yd = gmlp_branch(u, v, gmlp_ln_g, gmlp_ln_b, gmlp_ws, gmlp_b)
    branches = jnp.stack([ya, yb, yc, yd], axis=2)
    proj = jnp.einsum('bsgc,gcd->bsgd', branches, w_branch)
    gates = jax.nn.sigmoid((xn @ w_gate).reshape(B, S, N_BRANCH, D_MODEL))
    merged = jnp.einsum('bsgd,bsgd->bsd', gates, proj)
    return merged @ w_out


def swiglu(x, w1, w3, w2):
    return (jax.nn.silu(x @ w1) * (x @ w3)) @ w2


def moe_swiglu(xn, router, w1, w3, w2):
    B, S, D = xn.shape
    t = xn.reshape(B * S, D)
    logits = (t @ router).astype(jnp.float32)
    top_v, top_i = lax.top_k(logits, TOP_K)
    top_w = jax.nn.softmax(top_v, axis=-1)
    combine = jnp.sum(jax.nn.one_hot(top_i, N_EXPERTS, dtype=jnp.float32) * top_w[..., None],
                      axis=1).astype(t.dtype)
    y = jnp.zeros_like(t)
    for e in range(N_EXPERTS):
        y = y + combine[:, e:e + 1] * swiglu(t, w1[e], w3[e], w2[e])
    return y.reshape(B, S, D)


def setup_inputs(seed: int = 0) -> dict:
    key = jax.random.key(seed)
    keys = iter(jax.random.split(key, 128))
    f32 = jnp.float32

    def nrm(shape, scale):
        return jax.random.normal(next(keys), shape, f32) * scale

    def gain(n):
        return 1.0 + nrm((n,), 0.02)

    D, BW = D_MODEL, BRANCH_WIDTH
    out = {'x': nrm((BATCH, SEQ, D), 1.0)}
    for i in range(DEPTH):
        p = 'l%d_' % i
        out[p + 'norm1'] = gain(D)
        out[p + 'w_in'] = nrm((D, IN_COLS), D ** -0.5)
        out[p + 'w_gate'] = nrm((D, N_BRANCH * D), D ** -0.5)
        out[p + 'conv_w'] = nrm((CONV_WIDTH, BW), CONV_WIDTH ** -0.5)
        out[p + 'conv_b'] = nrm((BW,), 0.01)
        out[p + 'conv_ln_g'] = gain(BW)
        out[p + 'conv_ln_b'] = nrm((BW,), 0.01)
        out[p + 'pool_w'] = nrm((N_POOL_GROUPS, POOL_GROUP, POOL_GROUP), POOL_GROUP ** -0.5)
        out[p + 'pool_scale'] = gain(BW)
        out[p + 'gmlp_ln_g'] = gain(BW)
        out[p + 'gmlp_ln_b'] = nrm((BW,), 0.01)
        out[p + 'gmlp_ws'] = nrm((GMLP_HEADS, GMLP_CHUNK, GMLP_CHUNK), GMLP_CHUNK ** -0.5)
        out[p + 'gmlp_b'] = 1.0 + nrm((GMLP_HEADS, GMLP_CHUNK), 0.01)
        out[p + 'w_branch'] = nrm((N_BRANCH, BW, D), BW ** -0.5)
        out[p + 'w_out'] = nrm((D, D), D ** -0.5)
        out[p + 'norm2'] = gain(D)
        if i % 2 == 0:
            out[p + 'ffn_w1'] = nrm((D, D_FF), D ** -0.5)
            out[p + 'ffn_w3'] = nrm((D, D_FF), D ** -0.5)
            out[p + 'ffn_w2'] = nrm((D_FF, D), D_FF ** -0.5)
        else:
            out[p + 'router'] = nrm((D, N_EXPERTS), D ** -0.5)
            out[p + 'exp_w1'] = nrm((N_EXPERTS, D, D_FF_EXPERT), D ** -0.5)
            out[p + 'exp_w3'] = nrm((N_EXPERTS, D, D_FF_EXPERT), D ** -0.5)
            out[p + 'exp_w2'] = nrm((N_EXPERTS, D_FF_EXPERT, D), D_FF_EXPERT ** -0.5)
    out['final_norm'] = gain(D)
    return out


def reference(x,
              l0_norm1, l0_w_in, l0_w_gate, l0_conv_w, l0_conv_b, l0_conv_ln_g, l0_conv_ln_b,
              l0_pool_w, l0_pool_scale, l0_gmlp_ln_g, l0_gmlp_ln_b, l0_gmlp_ws, l0_gmlp_b,
              l0_w_branch, l0_w_out, l0_norm2, l0_ffn_w1, l0_ffn_w3, l0_ffn_w2,
              l1_norm1, l1_w_in, l1_w_gate, l1_conv_w, l1_conv_b, l1_conv_ln_g, l1_conv_ln_b,
              l1_pool_w, l1_pool_scale, l1_gmlp_ln_g, l1_gmlp_ln_b, l1_gmlp_ws, l1_gmlp_b,
              l1_w_branch, l1_w_out, l1_norm2, l1_router, l1_exp_w1, l1_exp_w3, l1_exp_w2,
              final_norm):
    norm1 = [l0_norm1, l1_norm1]
    mixer_params = [
        (l0_w_in, l0_w_gate, l0_conv_w, l0_conv_b, l0_conv_ln_g, l0_conv_ln_b, l0_pool_w,
         l0_pool_scale, l0_gmlp_ln_g, l0_gmlp_ln_b, l0_gmlp_ws, l0_gmlp_b, l0_w_branch, l0_w_out),
        (l1_w_in, l1_w_gate, l1_conv_w, l1_conv_b, l1_conv_ln_g, l1_conv_ln_b, l1_pool_w,
         l1_pool_scale, l1_gmlp_ln_g, l1_gmlp_ln_b, l1_gmlp_ws, l1_gmlp_b, l1_w_branch, l1_w_out),
    ]
    norm2 = [l0_norm2, l1_norm2]
    ffn_params = [(l0_ffn_w1, l0_ffn_w3, l0_ffn_w2),
                  (l1_router, l1_exp_w1, l1_exp_w3, l1_exp_w2)]
    h = x
    for i in range(DEPTH):
        h = h + hybrid_mixer(rmsnorm(h, norm1[i]), *mixer_params[i])
        hn = rmsnorm(h, norm2[i])
        if i % 2 == 0:
            h = h + swiglu(hn, *ffn_params[i])
        else:
            h = h + moe_swiglu(hn, *ffn_params[i])
    return rmsnorm(h, final_norm)
```

```python
import functools

import numpy as np
import jax
import jax.numpy as jnp
from jax import lax
from jax.experimental import pallas as pl
from jax.experimental.pallas import tpu as pltpu

F32 = jnp.float32
BF16 = jnp.bfloat16

N_BRANCH = 4
CONV_WIDTH = 31
CONV_PAD = CONV_WIDTH // 2
POOL_WINDOWS = (2, 4, 8, 16)
N_FFT_GROUPS = 4
GMLP_CHUNK = 128
GMLP_HEAD_DIM = 128
N_EXPERTS = 8
TOP_K = 2
EPS = 1e-6

HALO = 16
V7X_VMEM_BYTES = 64 * 1024 * 1024
VMEM_CAP = V7X_VMEM_BYTES - 6 * 1024 * 1024


def _vmem_limit(nbytes):
    return int(min(VMEM_CAP, nbytes + nbytes // 4 + (4 << 20)))


def _nbytes(shape, dtype):
    return int(np.prod(shape)) * jnp.dtype(dtype).itemsize


def _params(sem, nbytes):
    return pltpu.CompilerParams(dimension_semantics=sem, vmem_limit_bytes=_vmem_limit(nbytes))


def _pick(n, pref):
    t = min(n, pref)
    while n % t:
        t -= 8
    return t


def _rmsnorm_rows(x, g):
    return x * lax.rsqrt(jnp.mean(x * x, axis=-1, keepdims=True) + EPS) * g


def _rmsnorm_kernel(x_ref, g_ref, o_ref):
    o_ref[...] = _rmsnorm_rows(x_ref[...], g_ref[...]).astype(o_ref.dtype)


def rmsnorm(x, g, out_dtype):
    t, d = x.shape
    tm = _pick(t, 256)
    nb = 2 * _nbytes((tm, d), F32) + 2 * _nbytes((tm, d), out_dtype)
    return pl.pallas_call(
        _rmsnorm_kernel,
        grid=(t // tm,),
        in_specs=[pl.BlockSpec((tm, d), lambda i: (i, 0)), pl.BlockSpec((1, d), lambda i: (0, 0))],
        out_specs=pl.BlockSpec((tm, d), lambda i: (i, 0)),
        out_shape=jax.ShapeDtypeStruct((t, d), out_dtype),
        compiler_params=_params(("parallel",), nb),
        name="rmsnorm",
    )(x, g.reshape(1, d))


def _rmsnorm_router_kernel(x_ref, g_ref, r_ref, o_ref, idx_ref, wts_ref):
    y = _rmsnorm_rows(x_ref[...], g_ref[...])
    o_ref[...] = y
    logits = jnp.dot(y, r_ref[...], preferred_element_type=F32, precision=lax.Precision.HIGHEST)
    n_e = logits.shape[-1]
    lane = lax.broadcasted_iota(jnp.int32, logits.shape, 1)
    m1 = jnp.max(logits, axis=-1, keepdims=True)
    i1 = jnp.min(jnp.where(logits == m1, lane, n_e), axis=-1, keepdims=True)
    rest = jnp.where(lane == i1, -jnp.inf, logits)
    m2 = jnp.max(rest, axis=-1, keepdims=True)
    i2 = jnp.min(jnp.where(rest == m2, lane, n_e), axis=-1, keepdims=True)
    e2 = jnp.exp(m2 - m1)
    den = 1.0 + e2
    slot = lax.broadcasted_iota(jnp.int32, idx_ref.shape, 1)
    idx_ref[...] = jnp.where(slot == 0, i1, i2)
    wts_ref[...] = jnp.where(slot == 0, 1.0 / den, e2 / den)


def rmsnorm_router(x, g, router):
    t, d = x.shape
    n_e = router.shape[1]
    tm = _pick(t, 256)
    nb = 4 * _nbytes((tm, d), F32) + 2 * _nbytes((d, 128), F32)
    return pl.pallas_call(
        _rmsnorm_router_kernel,
        grid=(t // tm,),
        in_specs=[pl.BlockSpec((tm, d), lambda i: (i, 0)), pl.BlockSpec((1, d), lambda i: (0, 0)),
                  pl.BlockSpec((d, n_e), lambda i: (0, 0))],
        out_specs=[pl.BlockSpec((tm, d), lambda i: (i, 0)), pl.BlockSpec((tm, TOP_K), lambda i: (i, 0)),
                   pl.BlockSpec((tm, TOP_K), lambda i: (i, 0))],
        out_shape=[jax.ShapeDtypeStruct((t, d), F32), jax.ShapeDtypeStruct((t, TOP_K), jnp.int32),
                   jax.ShapeDtypeStruct((t, TOP_K), F32)],
        compiler_params=_params(("parallel",), nb),
        name="rmsnorm_router",
    )(x, g.reshape(1, d), router)


def _matmul_kernel(*refs, cast_w, has_res, epilogue):
    x_ref, w_ref = refs[0], refs[1]
    pos = 2
    r_ref = None
    if has_res:
        r_ref = refs[pos]
        pos += 1
    o_ref = refs[pos]
    if cast_w:
        wbf_ref = refs[pos + 1]

        @pl.when(pl.program_id(1) == 0)
        def _():
            wbf_ref[...] = w_ref[...].astype(BF16)

        w = wbf_ref[...]
    else:
        w = w_ref[...]
    acc = jnp.dot(x_ref[...], w, preferred_element_type=F32)
    if epilogue == "sigmoid":
        acc = jax.nn.sigmoid(acc)
    if has_res:
        acc = acc + r_ref[...]
    o_ref[...] = acc.astype(o_ref.dtype)


def matmul(x, w, *, out_dtype, res=None, epilogue=None, tm=1024, tn=512, k_block=None, w_lead=None):
    m, kx = x.shape
    kw, n = w.shape[-2], w.shape[-1]
    if k_block is None:
        assert kx == kw
        kk, tk = 0, kx
    else:
        kk, tk = k_block
    tm = _pick(m, tm)
    tn = _pick(n, tn)
    cast_w = w.dtype != BF16
    if w.ndim == 3:
        w_spec = pl.BlockSpec((None, tk, tn), lambda j, i: (w_lead, kk, j))
    else:
        w_spec = pl.BlockSpec((tk, tn), lambda j, i: (kk, j))
    in_specs = [pl.BlockSpec((tm, tk), lambda j, i: (i, kk)), w_spec]
    args = [x, w]
    nb = 2 * _nbytes((tm, tk), BF16) + 2 * _nbytes((tk, tn), w.dtype) + 2 * _nbytes((tm, tn), out_dtype)
    nb += _nbytes((tm, tn), F32)
    if res is not None:
        in_specs.append(pl.BlockSpec((tm, tn), lambda j, i: (i, j)))
        args.append(res)
        nb += 2 * _nbytes((tm, tn), res.dtype)
    scratch = []
    if cast_w:
        scratch.append(pltpu.VMEM((tk, tn), BF16))
        nb += _nbytes((tk, tn), BF16)
    return pl.pallas_call(
        functools.partial(_matmul_kernel, cast_w=cast_w, has_res=res is not None, epilogue=epilogue),
        grid=(n // tn, m // tm),
        in_specs=in_specs,
        out_specs=pl.BlockSpec((tm, tn), lambda j, i: (i, j)),
        out_shape=jax.ShapeDtypeStruct((m, n), out_dtype),
        scratch_shapes=scratch,
        compiler_params=_params(("arbitrary", "arbitrary"), nb),
        name="matmul",
    )(*args)


def _swiglu_up_kernel(x_ref, w1_ref, w3_ref, o_ref, w1bf_ref, w3bf_ref):
    @pl.when(pl.program_id(1) == 0)
    def _():
        w1bf_ref[...] = w1_ref[...].astype(BF16)
        w3bf_ref[...] = w3_ref[...].astype(BF16)

    x = x_ref[...]
    a = jnp.dot(x, w1bf_ref[...], preferred_element_type=F32)
    b = jnp.dot(x, w3bf_ref[...], preferred_element_type=F32)
    o_ref[...] = (jax.nn.silu(a) * b).astype(o_ref.dtype)


def swiglu_up(x, w1, w3, *, tm=1024, tn=256):
    m, k = x.shape
    n = w1.shape[1]
    tm = _pick(m, tm)
    tn = _pick(n, tn)
    nb = (2 * _nbytes((tm, k), BF16) + 4 * _nbytes((k, tn), F32) + 2 * _nbytes((k, tn), BF16)
          + 2 * _nbytes((tm, tn), BF16) + 3 * _nbytes((tm, tn), F32))
    return pl.pallas_call(
        _swiglu_up_kernel,
        grid=(n // tn, m // tm),
        in_specs=[pl.BlockSpec((tm, k), lambda j, i: (i, 0)), pl.BlockSpec((k, tn), lambda j, i: (0, j)),
                  pl.BlockSpec((k, tn), lambda j, i: (0, j))],
        out_specs=pl.BlockSpec((tm, tn), lambda j, i: (i, j)),
        out_shape=jax.ShapeDtypeStruct((m, n), BF16),
        scratch_shapes=[pltpu.VMEM((k, tn), BF16), pltpu.VMEM((k, tn), BF16)],
        compiler_params=_params(("arbitrary", "arbitrary"), nb),
        name="swiglu_up",
    )(x, w1, w3)


def _merge_kernel(ya_ref, yb_ref, yc_ref, yd_ref, g0_ref, g1_ref, g2_ref, g3_ref, wb_ref, o_ref, wbf_ref):
    @pl.when(pl.program_id(1) == 0)
    def _():
        wbf_ref[...] = wb_ref[...].astype(BF16)

    acc = None
    for g, (y_ref, g_ref) in enumerate(((ya_ref, g0_ref), (yb_ref, g1_ref), (yc_ref, g2_ref), (yd_ref, g3_ref))):
        proj = jnp.dot(y_ref[...], wbf_ref[g], preferred_element_type=F32)
        term = g_ref[...].astype(F32) * proj
        acc = term if acc is None else acc + term
    o_ref[...] = acc.astype(o_ref.dtype)


def merge_branches(branches, gates, w_branch, *, tm=512, tn=512):
    t, bw = branches[0].shape
    d = w_branch.shape[2]
    tm = _pick(t, tm)
    tn = _pick(d, tn)
    nj = d // tn
    y_spec = pl.BlockSpec((tm, bw), lambda j, i: (i, 0))
    gate_specs = [pl.BlockSpec((tm, tn), functools.partial(lambda j, i, g: (i, g * nj + j), g=g))
                  for g in range(N_BRANCH)]
    nb = (2 * N_BRANCH * _nbytes((tm, bw), BF16) + 2 * N_BRANCH * _nbytes((tm, tn), BF16)
          + 2 * _nbytes((N_BRANCH, bw, tn), F32) + _nbytes((N_BRANCH, bw, tn), BF16)
          + 2 * _nbytes((tm, tn), BF16) + 3 * _nbytes((tm, tn), F32))
    return pl.pallas_call(
        _merge_kernel,
        grid=(nj, t // tm),
        in_specs=[y_spec] * N_BRANCH + gate_specs + [pl.BlockSpec((N_BRANCH, bw, tn), lambda j, i: (0, 0, j))],
        out_specs=pl.BlockSpec((tm, tn), lambda j, i: (i, j)),
        out_shape=jax.ShapeDtypeStruct((t, d), BF16),
        scratch_shapes=[pltpu.VMEM((N_BRANCH, bw, tn), BF16)],
        compiler_params=_params(("arbitrary", "arbitrary"), nb),
        name="merge_branches",
    )(*branches, gates, gates, gates, gates, w_branch)


def _halo_specs(ts, width, col, seq_tiles_total):
    r = ts // HALO
    last = seq_tiles_total * r - 1
    prev = pl.BlockSpec((HALO, width), lambda i: (jnp.maximum(i * r - 1, 0), col))
    cur = pl.BlockSpec((ts, width), lambda i: (i, col))
    nxt = pl.BlockSpec((HALO, width), lambda i: (jnp.minimum((i + 1) * r, last), col))
    return [prev, cur, nxt]


def _edge_flags(ts, seq):
    tiles_per_seq = seq // ts
    k = pl.program_id(0) % tiles_per_seq
    return k == 0, k == tiles_per_seq - 1, k * ts


def _conv_kernel(lp_ref, lc_ref, ln_ref, gp_ref, gc_ref, gn_ref, cw_ref, cb_ref, lg_ref, lb_ref, o_ref,
                 abuf_ref, ybuf_ref, *, ts, seq):
    first, last, _ = _edge_flags(ts, seq)
    width = lc_ref.shape[1]
    ap = lp_ref[...] * jax.nn.sigmoid(gp_ref[...])
    an = ln_ref[...] * jax.nn.sigmoid(gn_ref[...])
    abuf_ref[0:HALO, :] = jnp.where(first, 0.0, ap)
    abuf_ref[HALO + ts:2 * HALO + ts, :] = jnp.where(last, 0.0, an)
    abuf_ref[HALO:HALO + ts, :] = lc_ref[...] * jax.nn.sigmoid(gc_ref[...])

    rows = 64 if ts % 64 == 0 else ts
    for r0 in range(0, ts, rows):
        for c0 in range(0, width, 128):
            acc = jnp.zeros((rows, 128), F32)
            for k in range(CONV_WIDTH):
                start = r0 + HALO - CONV_PAD + k
                acc = acc + cw_ref[k:k + 1, c0:c0 + 128] * abuf_ref[start:start + rows, c0:c0 + 128]
            ybuf_ref[r0:r0 + rows, c0:c0 + 128] = acc + cb_ref[:, c0:c0 + 128]

    y = ybuf_ref[...]
    mu = jnp.mean(y, axis=-1, keepdims=True)
    yc = y - mu
    var = jnp.mean(yc * yc, axis=-1, keepdims=True)
    z = yc * lax.rsqrt(var + EPS) * lg_ref[...] + lb_ref[...]
    o_ref[...] = jax.nn.silu(z).astype(o_ref.dtype)


def conv_branch(hin, conv_w, conv_b, ln_g, ln_b, *, seq, bw, ts=256):
    t = hin.shape[0]
    ts = _pick(seq, ts)
    n_tiles = t // ts
    row = lambda v: v.reshape(1, bw)
    const = lambda shape: pl.BlockSpec(shape, lambda i: (0, 0))
    nb = (4 * _nbytes((ts + 2 * HALO, bw), F32) + 2 * _nbytes((ts, bw), BF16) + 2 * _nbytes((ts + 2 * HALO, bw), F32)
          + 4 * _nbytes((ts, bw), F32))
    return pl.pallas_call(
        functools.partial(_conv_kernel, ts=ts, seq=seq),
        grid=(n_tiles,),
        in_specs=_halo_specs(ts, bw, 0, n_tiles) + _halo_specs(ts, bw, 1, n_tiles)
        + [const((CONV_WIDTH, bw)), const((1, bw)), const((1, bw)), const((1, bw))],
        out_specs=pl.BlockSpec((ts, bw), lambda i: (i, 0)),
        out_shape=jax.ShapeDtypeStruct((t, bw), BF16),
        scratch_shapes=[pltpu.VMEM((ts + 2 * HALO, bw), F32), pltpu.VMEM((ts, bw), F32)],
        compiler_params=_params(("parallel",), nb),
        name="conv_branch",
    )(hin, hin, hin, hin, hin, hin, conv_w, row(conv_b), row(ln_g), row(ln_b))


def _pool_kernel(pp_ref, pc_ref, pn_ref, pw_ref, ps_ref, o_ref, pbuf_ref, *, ts, seq):
    first, last, pos0 = _edge_flags(ts, seq)
    n_g = len(POOL_WINDOWS)
    pg = pc_ref.shape[1] // n_g
    pbuf_ref[0:HALO, :] = jnp.where(first, 0.0, pp_ref[...])
    pbuf_ref[HALO + ts:2 * HALO + ts, :] = jnp.where(last, 0.0, pn_ref[...])
    pbuf_ref[HALO:HALO + ts, :] = pc_ref[...]
    pos = pos0 + lax.broadcasted_iota(jnp.int32, (ts, 1), 0)
    for g, w in enumerate(POOL_WINDOWS):
        cols = slice(g * pg, (g + 1) * pg)
        win = None
        for d in range(-(w // 2), w // 2):
            term = pbuf_ref[HALO + d:HALO + d + ts, cols]
            win = term if win is None else win + term
        lo = jnp.maximum(pos - w // 2, 0)
        hi = jnp.minimum(pos + w // 2 - 1, seq - 1)
        cnt = (hi - lo + 1).astype(F32)
        pooled = win / cnt - pc_ref[:, cols]
        mixed = jnp.dot(pooled.astype(BF16), pw_ref[g].astype(BF16), preferred_element_type=F32)
        o_ref[:, cols] = (mixed * ps_ref[:, cols]).astype(o_ref.dtype)


def pool_branch(hin, pool_w, pool_scale, *, seq, bw, col, ts=256):
    t = hin.shape[0]
    ts = _pick(seq, ts)
    n_tiles = t // ts
    nb = 2 * _nbytes((ts + 2 * HALO, bw), F32) + 2 * _nbytes((ts, bw), BF16) + 4 * _nbytes((ts + 2 * HALO, bw), F32)
    nb += 2 * _nbytes(pool_w.shape, F32)
    return pl.pallas_call(
        functools.partial(_pool_kernel, ts=ts, seq=seq),
        grid=(n_tiles,),
        in_specs=_halo_specs(ts, bw, col, n_tiles)
        + [pl.BlockSpec(pool_w.shape, lambda i: (0, 0, 0)), pl.BlockSpec((1, bw), lambda i: (0, 0))],
        out_specs=pl.BlockSpec((ts, bw), lambda i: (i, 0)),
        out_shape=jax.ShapeDtypeStruct((t, bw), BF16),
        scratch_shapes=[pltpu.VMEM((ts + 2 * HALO, bw), F32)],
        compiler_params=_params(("parallel",), nb),
        name="pool_branch",
    )(hin, hin, hin, pool_w, pool_scale.reshape(1, bw))


def _dft_tables(seq, group):
    c = np.arange(group)
    ang = 2.0 * np.pi * np.outer(c, c) / group
    scale = 1.0 / np.sqrt(float(seq) * group)
    chan = np.concatenate([np.cos(ang), np.sin(ang)], axis=1) * scale
    fa = 64 if seq % 64 == 0 else 1
    sp = np.arange(seq)[:, None]
    ang_a = 2.0 * np.pi * ((sp * fa * np.arange(seq // fa)[None, :]) % seq) / seq
    ang_b = 2.0 * np.pi * ((sp * np.arange(fa)[None, :]) % seq) / seq
    ca, sa = jnp.asarray(np.cos(ang_a), F32)[:, :, None], jnp.asarray(np.sin(ang_a), F32)[:, :, None]
    cb, sb = jnp.asarray(np.cos(ang_b), F32)[:, None, :], jnp.asarray(np.sin(ang_b), F32)[:, None, :]
    cos_s = (ca * cb - sa * sb).reshape(seq, seq)
    sin_s = (sa * cb + ca * sb).reshape(seq, seq)
    pos = jnp.concatenate([cos_s, -sin_s], axis=1).astype(BF16)
    return jnp.asarray(chan, F32).astype(BF16), pos


def _chan_dft_kernel(f_ref, tbl_ref, o_ref):
    n_g = N_FFT_GROUPS
    fg = f_ref.shape[1] // n_g
    tbl = tbl_ref[...]
    for g in range(n_g):
        r = jnp.dot(f_ref[:, g * fg:(g + 1) * fg].astype(BF16), tbl, preferred_element_type=F32)
        o_ref[0, :, g * fg:(g + 1) * fg] = r[:, :fg].astype(o_ref.dtype)
        o_ref[1, :, g * fg:(g + 1) * fg] = r[:, fg:].astype(o_ref.dtype)


def fourier_branch(hin, chan_tbl, pos_tbl, *, batch, seq, bw, col, ts=512):
    t = hin.shape[0]
    ts = _pick(seq, ts)
    tiles_per_seq = seq // ts
    nb = 2 * _nbytes((ts, bw), F32) + 4 * _nbytes((ts, bw), BF16) + 2 * _nbytes(chan_tbl.shape, BF16)
    nb += 4 * _nbytes((ts, bw), F32)
    proj = pl.pallas_call(
        _chan_dft_kernel,
        grid=(t // ts,),
        in_specs=[pl.BlockSpec((ts, bw), lambda i: (i, col)), pl.BlockSpec(chan_tbl.shape, lambda i: (0, 0))],
        out_specs=pl.BlockSpec((None, 2, ts, bw), lambda i: (i // tiles_per_seq, 0, i % tiles_per_seq, 0)),
        out_shape=jax.ShapeDtypeStruct((batch, 2, seq, bw), BF16),
        compiler_params=_params(("parallel",), nb),
        name="fourier_channels",
    )(hin, chan_tbl)
    proj = proj.reshape(batch, 2 * seq, bw)
    outs = [matmul(pos_tbl, proj, out_dtype=BF16, tm=512, tn=512, w_lead=b) for b in range(batch)]
    return jnp.concatenate(outs, axis=0)


def _gmlp_kernel(u_ref, v_ref, lg_ref, lb_ref, ws_ref, bias_ref, o_ref, *, ts):
    v = v_ref[...]
    mu = jnp.mean(v, axis=-1, keepdims=True)
    vc = v - mu
    var = jnp.mean(vc * vc, axis=-1, keepdims=True)
    vn = (vc * lax.rsqrt(var + EPS) * lg_ref[...] + lb_ref[...]).astype(BF16)
    n_chunks = ts // GMLP_CHUNK
    n_heads = v.shape[1] // GMLP_HEAD_DIM
    for h in range(n_heads):
        cols = slice(h * GMLP_HEAD_DIM, (h + 1) * GMLP_HEAD_DIM)
        rhs = jnp.concatenate([vn[c * GMLP_CHUNK:(c + 1) * GMLP_CHUNK, cols] for c in range(n_chunks)], axis=1)
        s = jnp.dot(ws_ref[h].astype(BF16), rhs, preferred_element_type=F32)
        for c in range(n_chunks):
            rows = slice(c * GMLP_CHUNK, (c + 1) * GMLP_CHUNK)
            sc = s[:, c * GMLP_HEAD_DIM:(c + 1) * GMLP_HEAD_DIM] + bias_ref[:, cols]
            o_ref[rows, cols] = (u_ref[rows, cols] * sc).astype(o_ref.dtype)


def gmlp_branch(hin, ln_g, ln_b, ws, bias, *, bw, col_u, col_v, ts=512):
    t = hin.shape[0]
    ts = _pick(t, ts)
    assert ts % GMLP_CHUNK == 0
    n_heads = bw // GMLP_HEAD_DIM
    bias_rows = jnp.repeat(bias.T, GMLP_HEAD_DIM, axis=1)
    nb = 4 * _nbytes((ts, bw), F32) + 2 * _nbytes((ts, bw), BF16) + 6 * _nbytes((ts, bw), F32)
    return pl.pallas_call(
        functools.partial(_gmlp_kernel, ts=ts),
        grid=(t // ts,),
        in_specs=[pl.BlockSpec((ts, bw), lambda i: (i, col_u)), pl.BlockSpec((ts, bw), lambda i: (i, col_v)),
                  pl.BlockSpec((1, bw), lambda i: (0, 0)), pl.BlockSpec((1, bw), lambda i: (0, 0)),
                  pl.BlockSpec((n_heads, GMLP_CHUNK, GMLP_CHUNK), lambda i: (0, 0, 0)),
                  pl.BlockSpec((GMLP_CHUNK, bw), lambda i: (0, 0))],
        out_specs=pl.BlockSpec((ts, bw), lambda i: (i, 0)),
        out_shape=jax.ShapeDtypeStruct((t, bw), BF16),
        compiler_params=_params(("parallel",), nb),
        name="gmlp_branch",
    )(hin, hin, ln_g.reshape(1, bw), ln_b.reshape(1, bw), ws, bias_rows)


def _route(idx, tm):
    a = idx.size
    n_tiles = a // tm + N_EXPERTS
    e_flat = idx.reshape(a)
    onehot = (e_flat[:, None] == jnp.arange(N_EXPERTS, dtype=jnp.int32)[None, :]).astype(jnp.int32)
    csum = jnp.cumsum(onehot, axis=0)
    counts = csum[-1]
    rank = jnp.sum((csum - onehot) * onehot, axis=1)
    tiles_e = (counts + tm - 1) // tm
    tile_end = jnp.cumsum(tiles_e)
    tile_start = tile_end - tiles_e
    pos = jnp.sum(onehot * tile_start[None, :], axis=1) * tm + rank
    src_tok = jnp.zeros((n_tiles * tm,), jnp.int32).at[pos].set(jnp.arange(a, dtype=jnp.int32) // TOP_K)
    n_used = tile_end[-1]
    tile_ids = jnp.minimum(jnp.arange(n_tiles, dtype=jnp.int32), n_used - 1)
    tile_expert = jnp.sum((tile_ids[:, None] >= tile_end[None, :]).astype(jnp.int32), axis=1)
    changed = jnp.concatenate([jnp.ones((1,), jnp.int32),
                               (tile_expert[1:] != tile_expert[:-1]).astype(jnp.int32)])
    return pos, src_tok, tile_expert, changed, n_used.reshape(1).astype(jnp.int32)


def _row_copy(src_hbm, row, dst_ref, r, sem):
    return pltpu.make_async_copy(src_hbm.at[pl.ds(row, 1), :], dst_ref.at[pl.ds(r, 1), :], sem)


def _gather_rows_kernel(tok_ref, src_hbm, o_ref, buf_ref, sem, *, tg):
    base = pl.program_id(0) * tg

    def issue(r, c):
        _row_copy(src_hbm, tok_ref[base + r], buf_ref, r, sem).start()
        return c

    lax.fori_loop(0, tg, issue, 0)

    def drain(r, c):
        _row_copy(src_hbm, 0, buf_ref, r, sem).wait()
        return c

    lax.fori_loop(0, tg, drain, 0)
    o_ref[...] = buf_ref[...].astype(o_ref.dtype)


def gather_rows(src, tok, *, out_dtype, tg=256):
    p = tok.shape[0]
    d = src.shape[1]
    tg = _pick(p, tg)
    nb = 3 * _nbytes((tg, d), F32) + 2 * _nbytes((tg, d), out_dtype)
    return pl.pallas_call(
        functools.partial(_gather_rows_kernel, tg=tg),
        grid_spec=pltpu.PrefetchScalarGridSpec(
            num_scalar_prefetch=1,
            grid=(p // tg,),
            in_specs=[pl.BlockSpec(memory_space=pl.ANY)],
            out_specs=pl.BlockSpec((tg, d), lambda i, tok: (i, 0)),
            scratch_shapes=[pltpu.VMEM((tg, d), F32), pltpu.SemaphoreType.DMA(())],
        ),
        out_shape=jax.ShapeDtypeStruct((p, d), out_dtype),
        compiler_params=_params(("arbitrary",), nb),
        name="gather_rows",
    )(tok, src)


def _moe_up_kernel(te_ref, ch_ref, nu_ref, x_ref, w1_ref, w3_ref, o_ref, w1bf_ref, w3bf_ref):
    n = pl.program_id(1)

    @pl.when(ch_ref[n] == 1)
    def _():
        w1bf_ref[...] = w1_ref[...].astype(BF16)
        w3bf_ref[...] = w3_ref[...].astype(BF16)

    @pl.when(n < nu_ref[0])
    def _():
        x = x_ref[...]
        a = jnp.dot(x, w1bf_ref[...], preferred_element_type=F32)
        b = jnp.dot(x, w3bf_ref[...], preferred_element_type=F32)
        o_ref[...] = (jax.nn.silu(a) * b).astype(o_ref.dtype)

    @pl.when(n >= nu_ref[0])
    def _():
        o_ref[...] = jnp.zeros_like(o_ref)


def _moe_down_kernel(te_ref, ch_ref, nu_ref, x_ref, w_ref, o_ref, wbf_ref):
    n = pl.program_id(1)

    @pl.when(ch_ref[n] == 1)
    def _():
        wbf_ref[...] = w_ref[...].astype(BF16)

    @pl.when(n < nu_ref[0])
    def _():
        o_ref[...] = jnp.dot(x_ref[...], wbf_ref[...], preferred_element_type=F32).astype(o_ref.dtype)

    @pl.when(n >= nu_ref[0])
    def _():
        o_ref[...] = jnp.zeros_like(o_ref)


def _moe_specs(tm, k, tn):
    row = lambda j, n, te, ch, nu: (jnp.minimum(n, nu[0] - 1), 0)
    out = lambda j, n, te, ch, nu: (n, j)
    wgt = lambda j, n, te, ch, nu: (te[n], 0, j)
    return pl.BlockSpec((tm, k), row), pl.BlockSpec((None, k, tn), wgt), pl.BlockSpec((tm, tn), out)


def moe_up(xs, route, w1, w3, *, tm, tn=256):
    p, k = xs.shape
    f = w1.shape[2]
    tn = _pick(f, tn)
    x_spec, w_spec, o_spec = _moe_specs(tm, k, tn)
    nb = (2 * _nbytes((tm, k), BF16) + 4 * _nbytes((k, tn), F32) + 2 * _nbytes((k, tn), BF16)
          + 2 * _nbytes((tm, tn), BF16) + 3 * _nbytes((tm, tn), F32))
    return pl.pallas_call(
        _moe_up_kernel,
        grid_spec=pltpu.PrefetchScalarGridSpec(
            num_scalar_prefetch=3,
            grid=(f // tn, p // tm),
            in_specs=[x_spec, w_spec, w_spec],
            out_specs=o_spec,
            scratch_shapes=[pltpu.VMEM((k, tn), BF16), pltpu.VMEM((k, tn), BF16)],
        ),
        out_shape=jax.ShapeDtypeStruct((p, f), BF16),
        compiler_params=_params(("arbitrary", "arbitrary"), nb),
        name="moe_up",
    )(*route, xs, w1, w3)


def moe_down(acts, route, w2, *, tm, tn=512):
    p, k = acts.shape
    d = w2.shape[2]
    tn = _pick(d, tn)
    x_spec, w_spec, o_spec = _moe_specs(tm, k, tn)
    nb = (2 * _nbytes((tm, k), BF16) + 2 * _nbytes((k, tn), F32) + _nbytes((k, tn), BF16)
          + 3 * _nbytes((tm, tn), F32))
    return pl.pallas_call(
        _moe_down_kernel,
        grid_spec=pltpu.PrefetchScalarGridSpec(
            num_scalar_prefetch=3,
            grid=(d // tn, p // tm),
            in_specs=[x_spec, w_spec],
            out_specs=o_spec,
            scratch_shapes=[pltpu.VMEM((k, tn), BF16)],
        ),
        out_shape=jax.ShapeDtypeStruct((p, d), F32),
        compiler_params=_params(("arbitrary", "arbitrary"), nb),
        name="moe_down",
    )(*route, acts, w2)


def _combine_norm_kernel(pos_ref, h_ref, wts_ref, ys_hbm, g_ref, o_ref, buf_ref, sem, *, tg):
    base = pl.program_id(0) * tg

    def issue(r, c):
        for k in range(TOP_K):
            _row_copy(ys_hbm, pos_ref[(base + r) * TOP_K + k], buf_ref.at[k], r, sem).start()
        return c

    lax.fori_loop(0, tg, issue, 0)

    def drain(r, c):
        for k in range(TOP_K):
            _row_copy(ys_hbm, 0, buf_ref.at[k], r, sem).wait()
        return c

    lax.fori_loop(0, tg, drain, 0)
    h = h_ref[...]
    for k in range(TOP_K):
        h = h + wts_ref[:, k:k + 1] * buf_ref[k]
    o_ref[...] = _rmsnorm_rows(h, g_ref[...]).astype(o_ref.dtype)


def combine_norm(h, wts, pos, ys, g, *, tg=128):
    t, d = h.shape
    tg = _pick(t, tg)
    nb = 4 * _nbytes((tg, d), F32) + TOP_K * _nbytes((tg, d), F32) + 4 * _nbytes((tg, d), F32)
    return pl.pallas_call(
        functools.partial(_combine_norm_kernel, tg=tg),
        grid_spec=pltpu.PrefetchScalarGridSpec(
            num_scalar_prefetch=1,
            grid=(t // tg,),
            in_specs=[pl.BlockSpec((tg, d), lambda i, pos: (i, 0)), pl.BlockSpec((tg, TOP_K), lambda i, pos: (i, 0)),
                      pl.BlockSpec(memory_space=pl.ANY), pl.BlockSpec((1, d), lambda i, pos: (0, 0))],
            out_specs=pl.BlockSpec((tg, d), lambda i, pos: (i, 0)),
            scratch_shapes=[pltpu.VMEM((TOP_K, tg, d), F32), pltpu.SemaphoreType.DMA(())],
        ),
        out_shape=jax.ShapeDtypeStruct((t, d), F32),
        compiler_params=_params(("arbitrary",), nb),
        name="combine_norm",
    )(pos, h, wts, ys, g.reshape(1, d))


def moe_ffn_then_norm(h, norm_g, router, w1, w3, w2, final_g, *, tm=512):
    t = h.shape[0]
    tm = _pick(t * TOP_K, tm)
    hn, idx, wts = rmsnorm_router(h, norm_g, router)
    pos, src_tok, tile_expert, changed, n_used = _route(idx, tm)
    route = (tile_expert, changed, n_used)
    xs = gather_rows(hn, src_tok, out_dtype=BF16)
    acts = moe_up(xs, route, w1, w3, tm=tm)
    ys = moe_down(acts, route, w2, tm=tm)
    return combine_norm(h, wts, pos, ys, final_g)


def _mixer(h, norm1, w_in, w_gate, conv_w, conv_b, conv_ln_g, conv_ln_b, pool_w, pool_scale, gmlp_ln_g, gmlp_ln_b,
           gmlp_ws, gmlp_b, w_branch, w_out, tables, *, batch, seq):
    bw = w_branch.shape[1]
    xn = rmsnorm(h, norm1, BF16)
    hin = matmul(xn, w_in, out_dtype=F32)
    ya = conv_branch(hin, conv_w, conv_b, conv_ln_g, conv_ln_b, seq=seq, bw=bw)
    yb = pool_branch(hin, pool_w, pool_scale, seq=seq, bw=bw, col=2)
    yc = fourier_branch(hin, *tables, batch=batch, seq=seq, bw=bw, col=3)
    yd = gmlp_branch(hin, gmlp_ln_g, gmlp_ln_b, gmlp_ws, gmlp_b, bw=bw, col_u=4, col_v=5)
    gates = matmul(xn, w_gate, out_dtype=BF16, epilogue="sigmoid")
    merged = merge_branches((ya, yb, yc, yd), gates, w_branch)
    return matmul(merged, w_out, out_dtype=F32, res=h)


def kernel(x, l0_norm1, l0_w_in, l0_w_gate, l0_conv_w, l0_conv_b, l0_conv_ln_g, l0_conv_ln_b, l0_pool_w, l0_pool_scale, l0_gmlp_ln_g, l0_gmlp_ln_b, l0_gmlp_ws, l0_gmlp_b, l0_w_branch, l0_w_out, l0_norm2, l0_ffn_w1, l0_ffn_w3, l0_ffn_w2, l1_norm1, l1_w_in, l1_w_gate, l1_conv_w, l1_conv_b, l1_conv_ln_g, l1_conv_ln_b, l1_pool_w, l1_pool_scale, l1_gmlp_ln_g, l1_gmlp_ln_b, l1_gmlp_ws, l1_gmlp_b, l1_w_branch, l1_w_out, l1_norm2, l1_router, l1_exp_w1, l1_exp_w3, l1_exp_w2, final_norm):
    batch, seq, d = x.shape
    bw = l0_w_branch.shape[1]
    tables = _dft_tables(seq, bw // N_FFT_GROUPS)
    h = x.reshape(batch * seq, d)

    h = _mixer(h, l0_norm1, l0_w_in, l0_w_gate, l0_conv_w, l0_conv_b, l0_conv_ln_g, l0_conv_ln_b, l0_pool_w,
               l0_pool_scale, l0_gmlp_ln_g, l0_gmlp_ln_b, l0_gmlp_ws, l0_gmlp_b, l0_w_branch, l0_w_out, tables,
               batch=batch, seq=seq)
    hn = rmsnorm(h, l0_norm2, BF16)
    acts = swiglu_up(hn, l0_ffn_w1, l0_ffn_w3)
    half = l0_ffn_w2.shape[0] // 2
    h = matmul(acts, l0_ffn_w2, out_dtype=F32, res=h, tm=512, k_block=(0, half))
    h = matmul(acts, l0_ffn_w2, out_dtype=F32, res=h, tm=512, k_block=(1, half))

    h = _mixer(h, l1_norm1, l1_w_in, l1_w_gate, l1_conv_w, l1_conv_b, l1_conv_ln_g, l1_conv_ln_b, l1_pool_w,
               l1_pool_scale, l1_gmlp_ln_g, l1_gmlp_ln_b, l1_gmlp_ws, l1_gmlp_b, l1_w_branch, l1_w_out, tables,
               batch=batch, seq=seq)
    out = moe_ffn_then_norm(h, l1_norm2, l1_router, l1_exp_w1, l1_exp_w3, l1_exp_w2, final_norm)
    return out.reshape(batch, seq, d)
```

```python
import functools

import numpy as np
import jax
import jax.numpy as jnp
from jax import lax
from jax.experimental import pallas as pl
from jax.experimental.pallas import tpu as pltpu

F32 = jnp.float32
BF16 = jnp.bfloat16

N_BRANCH = 4
CONV_WIDTH = 31
CONV_PAD = CONV_WIDTH // 2
POOL_WINDOWS = (2, 4, 8, 16)
N_FFT_GROUPS = 4
GMLP_CHUNK = 128
GMLP_HEAD_DIM = 128
N_EXPERTS = 8
TOP_K = 2
EPS = 1e-6

HALO = 16
V7X_VMEM_BYTES = 64 * 1024 * 1024
VMEM_CAP = V7X_VMEM_BYTES - 6 * 1024 * 1024


def _vmem_limit(nbytes):
    return int(min(VMEM_CAP, nbytes + nbytes // 4 + (4 << 20)))


def _nbytes(shape, dtype):
    return int(np.prod(shape)) * jnp.dtype(dtype).itemsize


def _params(sem, nbytes):
    return pltpu.CompilerParams(dimension_semantics=sem, vmem_limit_bytes=_vmem_limit(nbytes))


def _pick(n, pref):
    t = min(n, pref)
    while n % t:
        t -= 8
    return t


def _rmsnorm_rows(x, g):
    return x * lax.rsqrt(jnp.mean(x * x, axis=-1, keepdims=True) + EPS) * g


def _rmsnorm_kernel(x_ref, g_ref, o_ref):
    o_ref[...] = _rmsnorm_rows(x_ref[...], g_ref[...]).astype(o_ref.dtype)


def rmsnorm(x, g, out_dtype):
    t, d = x.shape
    tm = _pick(t, 256)
    nb = 2 * _nbytes((tm, d), F32) + 2 * _nbytes((tm, d), out_dtype)
    return pl.pallas_call(
        _rmsnorm_kernel,
        grid=(t // tm,),
        in_specs=[pl.BlockSpec((tm, d), lambda i: (i, 0)), pl.BlockSpec((1, d), lambda i: (0, 0))],
        out_specs=pl.BlockSpec((tm, d), lambda i: (i, 0)),
        out_shape=jax.ShapeDtypeStruct((t, d), out_dtype),
        compiler_params=_params(("parallel",), nb),
        name="rmsnorm",
    )(x, g.reshape(1, d))


def _rmsnorm_router_kernel(x_ref, g_ref, r_ref, o_ref, idx_ref, wts_ref):
    y = _rmsnorm_rows(x_ref[...], g_ref[...])
    o_ref[...] = y
    logits = jnp.dot(y, r_ref[...], preferred_element_type=F32, precision=lax.Precision.HIGHEST)
    n_e = logits.shape[-1]
    lane = lax.broadcasted_iota(jnp.int32, logits.shape, 1)
    m1 = jnp.max(logits, axis=-1, keepdims=True)
    i1 = jnp.min(jnp.where(logits == m1, lane, n_e), axis=-1, keepdims=True)
    rest = jnp.where(lane == i1, -jnp.inf, logits)
    m2 = jnp.max(rest, axis=-1, keepdims=True)
    i2 = jnp.min(jnp.where(rest == m2, lane, n_e), axis=-1, keepdims=True)
    e2 = jnp.exp(m2 - m1)
    den = 1.0 + e2
    slot = lax.broadcasted_iota(jnp.int32, idx_ref.shape, 1)
    idx_ref[...] = jnp.where(slot == 0, i1, i2)
    wts_ref[...] = jnp.where(slot == 0, 1.0 / den, e2 / den)


def rmsnorm_router(x, g, router):
    t, d = x.shape
    n_e = router.shape[1]
    tm = _pick(t, 256)
    nb = 4 * _nbytes((tm, d), F32) + 2 * _nbytes((d, 128), F32)
    return pl.pallas_call(
        _rmsnorm_router_kernel,
        grid=(t // tm,),
        in_specs=[pl.BlockSpec((tm, d), lambda i: (i, 0)), pl.BlockSpec((1, d), lambda i: (0, 0)),
                  pl.BlockSpec((d, n_e), lambda i: (0, 0))],
        out_specs=[pl.BlockSpec((tm, d), lambda i: (i, 0)), pl.BlockSpec((tm, TOP_K), lambda i: (i, 0)),
                   pl.BlockSpec((tm, TOP_K), lambda i: (i, 0))],
        out_shape=[jax.ShapeDtypeStruct((t, d), F32), jax.ShapeDtypeStruct((t, TOP_K), jnp.int32),
                   jax.ShapeDtypeStruct((t, TOP_K), F32)],
        compiler_params=_params(("parallel",), nb),
        name="rmsnorm_router",
    )(x, g.reshape(1, d), router)


def _matmul_kernel(*refs, cast_w, has_res, epilogue):
    x_ref, w_ref = refs[0], refs[1]
    pos = 2
    r_ref = None
    if has_res:
        r_ref = refs[pos]
        pos += 1
    o_ref = refs[pos]
    if cast_w:
        wbf_ref = refs[pos + 1]

        @pl.when(pl.program_id(1) == 0)
        def _():
            wbf_ref[...] = w_ref[...].astype(BF16)

        w = wbf_ref[...]
    else:
        w = w_ref[...]
    acc = jnp.dot(x_ref[...], w, preferred_element_type=F32)
    if epilogue == "sigmoid":
        acc = jax.nn.sigmoid(acc)
    if has_res:
        acc = acc + r_ref[...]
    o_ref[...] = acc.astype(o_ref.dtype)


def matmul(x, w, *, out_dtype, res=None, epilogue=None, tm=1024, tn=512, k_block=None, w_lead=None):
    m, kx = x.shape
    kw, n = w.shape[-2], w.shape[-1]
    if k_block is None:
        assert kx == kw
        kk, tk = 0, kx
    else:
        kk, tk = k_block
    tm = _pick(m, tm)
    tn = _pick(n, tn)
    cast_w = w.dtype != BF16
    if w.ndim == 3:
        w_spec = pl.BlockSpec((None, tk, tn), lambda j, i: (w_lead, kk, j))
    else:
        w_spec = pl.BlockSpec((tk, tn), lambda j, i: (kk, j))
    in_specs = [pl.BlockSpec((tm, tk), lambda j, i: (i, kk)), w_spec]
    args = [x, w]
    nb = 2 * _nbytes((tm, tk), BF16) + 2 * _nbytes((tk, tn), w.dtype) + 2 * _nbytes((tm, tn), out_dtype)
    nb += _nbytes((tm, tn), F32)
    if res is not None:
        in_specs.append(pl.BlockSpec((tm, tn), lambda j, i: (i, j)))
        args.append(res)
        nb += 2 * _nbytes((tm, tn), res.dtype)
    scratch = []
    if cast_w:
        scratch.append(pltpu.VMEM((tk, tn), BF16))
        nb += _nbytes((tk, tn), BF16)
    return pl.pallas_call(
        functools.partial(_matmul_kernel, cast_w=cast_w, has_res=res is not None, epilogue=epilogue),
        grid=(n // tn, m // tm),
        in_specs=in_specs,
        out_specs=pl.BlockSpec((tm, tn), lambda j, i: (i, j)),
        out_shape=jax.ShapeDtypeStruct((m, n), out_dtype),
        scratch_shapes=scratch,
        compiler_params=_params(("arbitrary", "arbitrary"), nb),
        name="matmul",
    )(*args)


def _swiglu_up_kernel(x_ref, w1_ref, w3_ref, o_ref, w1bf_ref, w3bf_ref):
    @pl.when(pl.program_id(1) == 0)
    def _():
        w1bf_ref[...] = w1_ref[...].astype(BF16)
        w3bf_ref[...] = w3_ref[...].astype(BF16)

    x = x_ref[...]
    a = jnp.dot(x, w1bf_ref[...], preferred_element_type=F32)
    b = jnp.dot(x, w3bf_ref[...], preferred_element_type=F32)
    o_ref[...] = (jax.nn.silu(a) * b).astype(o_ref.dtype)


def swiglu_up(x, w1, w3, *, tm=1024, tn=256):
    m, k = x.shape
    n = w1.shape[1]
    tm = _pick(m, tm)
    tn = _pick(n, tn)
    nb = (2 * _nbytes((tm, k), BF16) + 4 * _nbytes((k, tn), F32) + 2 * _nbytes((k, tn), BF16)
          + 2 * _nbytes((tm, tn), BF16) + 3 * _nbytes((tm, tn), F32))
    return pl.pallas_call(
        _swiglu_up_kernel,
        grid=(n // tn, m // tm),
        in_specs=[pl.BlockSpec((tm, k), lambda j, i: (i, 0)), pl.BlockSpec((k, tn), lambda j, i: (0, j)),
                  pl.BlockSpec((k, tn), lambda j, i: (0, j))],
        out_specs=pl.BlockSpec((tm, tn), lambda j, i: (i, j)),
        out_shape=jax.ShapeDtypeStruct((m, n), BF16),
        scratch_shapes=[pltpu.VMEM((k, tn), BF16), pltpu.VMEM((k, tn), BF16)],
        compiler_params=_params(("arbitrary", "arbitrary"), nb),
        name="swiglu_up",
    )(x, w1, w3)


def _merge_kernel(ya_ref, yb_ref, yc_ref, yd_ref, g0_ref, g1_ref, g2_ref, g3_ref, wb_ref, o_ref, wbf_ref):
    @pl.when(pl.program_id(1) == 0)
    def _():
        wbf_ref[...] = wb_ref[...].astype(BF16)

    acc = None
    for g, (y_ref, g_ref) in enumerate(((ya_ref, g0_ref), (yb_ref, g1_ref), (yc_ref, g2_ref), (yd_ref, g3_ref))):
        proj = jnp.dot(y_ref[...], wbf_ref[g], preferred_element_type=F32)
        term = g_ref[...].astype(F32) * proj
        acc = term if acc is None else acc + term
    o_ref[...] = acc.astype(o_ref.dtype)


def merge_branches(branches, gates, w_branch, *, tm=512, tn=512):
    t, bw = branches[0].shape
    d = w_branch.shape[2]
    tm = _pick(t, tm)
    tn = _pick(d, tn)
    nj = d // tn
    y_spec = pl.BlockSpec((tm, bw), lambda j, i: (i, 0))
    gate_specs = [pl.BlockSpec((tm, tn), functools.partial(lambda j, i, g: (i, g * nj + j), g=g))
                  for g in range(N_BRANCH)]
    nb = (2 * N_BRANCH * _nbytes((tm, bw), BF16) + 2 * N_BRANCH * _nbytes((tm, tn), BF16)
          + 2 * _nbytes((N_BRANCH, bw, tn), F32) + _nbytes((N_BRANCH, bw, tn), BF16)
          + 2 * _nbytes((tm, tn), BF16) + 3 * _nbytes((tm, tn), F32))
    return pl.pallas_call(
        _merge_kernel,
        grid=(nj, t // tm),
        in_specs=[y_spec] * N_BRANCH + gate_specs + [pl.BlockSpec((N_BRANCH, bw, tn), lambda j, i: (0, 0, j))],
        out_specs=pl.BlockSpec((tm, tn), lambda j, i: (i, j)),
        out_shape=jax.ShapeDtypeStruct((t, d), BF16),
        scratch_shapes=[pltpu.VMEM((N_BRANCH, bw, tn), BF16)],
        compiler_params=_params(("arbitrary", "arbitrary"), nb),
        name="merge_branches",
    )(*branches, gates, gates, gates, gates, w_branch)


def _halo_specs(ts, width, col, seq_tiles_total):
    r = ts // HALO
    last = seq_tiles_total * r - 1
    prev = pl.BlockSpec((HALO, width), lambda i: (jnp.maximum(i * r - 1, 0), col))
    cur = pl.BlockSpec((ts, width), lambda i: (i, col))
    nxt = pl.BlockSpec((HALO, width), lambda i: (jnp.minimum((i + 1) * r, last), col))
    return [prev, cur, nxt]


def _edge_flags(ts, seq):
    tiles_per_seq = seq // ts
    k = pl.program_id(0) % tiles_per_seq
    return k == 0, k == tiles_per_seq - 1, k * ts


def _conv_kernel(lp_ref, lc_ref, ln_ref, gp_ref, gc_ref, gn_ref, cw_ref, cb_ref, lg_ref, lb_ref, o_ref,
                 abuf_ref, ybuf_ref, *, ts, seq):
    first, last, _ = _edge_flags(ts, seq)
    width = lc_ref.shape[1]
    ap = lp_ref[...] * jax.nn.sigmoid(gp_ref[...])
    an = ln_ref[...] * jax.nn.sigmoid(gn_ref[...])
    abuf_ref[0:HALO, :] = jnp.where(first, 0.0, ap)
    abuf_ref[HALO + ts:2 * HALO + ts, :] = jnp.where(last, 0.0, an)
    abuf_ref[HALO:HALO + ts, :] = lc_ref[...] * jax.nn.sigmoid(gc_ref[...])

    rows = 64 if ts % 64 == 0 else ts
    for r0 in range(0, ts, rows):
        for c0 in range(0, width, 128):
            acc = jnp.zeros((rows, 128), F32)
            for k in range(CONV_WIDTH):
                start = r0 + HALO - CONV_PAD + k
                acc = acc + cw_ref[k:k + 1, c0:c0 + 128] * abuf_ref[start:start + rows, c0:c0 + 128]
            ybuf_ref[r0:r0 + rows, c0:c0 + 128] = acc + cb_ref[:, c0:c0 + 128]

    y = ybuf_ref[...]
    mu = jnp.mean(y, axis=-1, keepdims=True)
    yc = y - mu
    var = jnp.mean(yc * yc, axis=-1, keepdims=True)
    z = yc * lax.rsqrt(var + EPS) * lg_ref[...] + lb_ref[...]
    o_ref[...] = jax.nn.silu(z).astype(o_ref.dtype)


def conv_branch(hin, conv_w, conv_b, ln_g, ln_b, *, seq, bw, ts=256):
    t = hin.shape[0]
    ts = _pick(seq, ts)
    n_tiles = t // ts
    row = lambda v: v.reshape(1, bw)
    const = lambda shape: pl.BlockSpec(shape, lambda i: (0, 0))
    nb = (4 * _nbytes((ts + 2 * HALO, bw), F32) + 2 * _nbytes((ts, bw), BF16) + 2 * _nbytes((ts + 2 * HALO, bw), F32)
          + 4 * _nbytes((ts, bw), F32))
    return pl.pallas_call(
        functools.partial(_conv_kernel, ts=ts, seq=seq),
        grid=(n_tiles,),
        in_specs=_halo_specs(ts, bw, 0, n_tiles) + _halo_specs(ts, bw, 1, n_tiles)
        + [const((CONV_WIDTH, bw)), const((1, bw)), const((1, bw)), const((1, bw))],
        out_specs=pl.BlockSpec((ts, bw), lambda i: (i, 0)),
        out_shape=jax.ShapeDtypeStruct((t, bw), BF16),
        scratch_shapes=[pltpu.VMEM((ts + 2 * HALO, bw), F32), pltpu.VMEM((ts, bw), F32)],
        compiler_params=_params(("parallel",), nb),
        name="conv_branch",
    )(hin, hin, hin, hin, hin, hin, conv_w, row(conv_b), row(ln_g), row(ln_b))


def _pool_kernel(pp_ref, pc_ref, pn_ref, pw_ref, ps_ref, o_ref, pbuf_ref, *, ts, seq):
    first, last, pos0 = _edge_flags(ts, seq)
    n_g = len(POOL_WINDOWS)
    pg = pc_ref.shape[1] // n_g
    pbuf_ref[0:HALO, :] = jnp.where(first, 0.0, pp_ref[...])
    pbuf_ref[HALO + ts:2 * HALO + ts, :] = jnp.where(last, 0.0, pn_ref[...])
    pbuf_ref[HALO:HALO + ts, :] = pc_ref[...]
    pos = pos0 + lax.broadcasted_iota(jnp.int32, (ts, 1), 0)
    for g, w in enumerate(POOL_WINDOWS):
        cols = slice(g * pg, (g + 1) * pg)
        win = None
        for d in range(-(w // 2), w // 2):
            term = pbuf_ref[HALO + d:HALO + d + ts, cols]
            win = term if win is None else win + term
        lo = jnp.maximum(pos - w // 2, 0)
        hi = jnp.minimum(pos + w // 2 - 1, seq - 1)
        cnt = (hi - lo + 1).astype(F32)
        pooled = win / cnt - pc_ref[:, cols]
        mixed = jnp.dot(pooled.astype(BF16), pw_ref[g].astype(BF16), preferred_element_type=F32)
        o_ref[:, cols] = (mixed * ps_ref[:, cols]).astype(o_ref.dtype)


def pool_branch(hin, pool_w, pool_scale, *, seq, bw, col, ts=256):
    t = hin.shape[0]
    ts = _pick(seq, ts)
    n_tiles = t // ts
    nb = 2 * _nbytes((ts + 2 * HALO, bw), F32) + 2 * _nbytes((ts, bw), BF16) + 4 * _nbytes((ts + 2 * HALO, bw), F32)
    nb += 2 * _nbytes(pool_w.shape, F32)
    return pl.pallas_call(
        functools.partial(_pool_kernel, ts=ts, seq=seq),
        grid=(n_tiles,),
        in_specs=_halo_specs(ts, bw, col, n_tiles)
        + [pl.BlockSpec(pool_w.shape, lambda i: (0, 0, 0)), pl.BlockSpec((1, bw), lambda i: (0, 0))],
        out_specs=pl.BlockSpec((ts, bw), lambda i: (i, 0)),
        out_shape=jax.ShapeDtypeStruct((t, bw), BF16),
        scratch_shapes=[pltpu.VMEM((ts + 2 * HALO, bw), F32)],
        compiler_params=_params(("parallel",), nb),
        name="pool_branch",
    )(hin, hin, hin, pool_w, pool_scale.reshape(1, bw))


def _dft_tables(seq, group):
    c = np.arange(group)
    ang = 2.0 * np.pi * np.outer(c, c) / group
    scale = 1.0 / np.sqrt(float(seq) * group)
    chan = np.concatenate([np.cos(ang), np.sin(ang)], axis=1) * scale
    fa = 64 if seq % 64 == 0 else 1
    sp = np.arange(seq)[:, None]
    ang_a = 2.0 * np.pi * ((sp * fa * np.arange(seq // fa)[None, :]) % seq) / seq
    ang_b = 2.0 * np.pi * ((sp * np.arange(fa)[None, :]) % seq) / seq
    ca, sa = jnp.asarray(np.cos(ang_a), F32)[:, :, None], jnp.asarray(np.sin(ang_a), F32)[:, :, None]
    cb, sb = jnp.asarray(np.cos(ang_b), F32)[:, None, :], jnp.asarray(np.sin(ang_b), F32)[:, None, :]
    cos_s = (ca * cb - sa * sb).reshape(seq, seq)
    sin_s = (sa * cb + ca * sb).reshape(seq, seq)
    pos = jnp.concatenate([cos_s, -sin_s], axis=1).astype(BF16)
    return jnp.asarray(chan, F32).astype(BF16), pos


def _chan_dft_kernel(f_ref, tbl_ref, o_ref):
    n_g = N_FFT_GROUPS
    fg = f_ref.shape[1] // n_g
    tbl = tbl_ref[...]
    for g in range(n_g):
        r = jnp.dot(f_ref[:, g * fg:(g + 1) * fg].astype(BF16), tbl, preferred_element_type=F32)
        o_ref[0, :, g * fg:(g + 1) * fg] = r[:, :fg].astype(o_ref.dtype)
        o_ref[1, :, g * fg:(g + 1) * fg] = r[:, fg:].astype(o_ref.dtype)


def fourier_branch(hin, chan_tbl, pos_tbl, *, batch, seq, bw, col, ts=512):
    t = hin.shape[0]
    ts = _pick(seq, ts)
    tiles_per_seq = seq // ts
    nb = 2 * _nbytes((ts, bw), F32) + 4 * _nbytes((ts, bw), BF16) + 2 * _nbytes(chan_tbl.shape, BF16)
    nb += 4 * _nbytes((ts, bw), F32)
    proj = pl.pallas_call(
        _chan_dft_kernel,
        grid=(t // ts,),
        in_specs=[pl.BlockSpec((ts, bw), lambda i: (i, col)), pl.BlockSpec(chan_tbl.shape, lambda i: (0, 0))],
        out_specs=pl.BlockSpec((None, 2, ts, bw), lambda i: (i // tiles_per_seq, 0, i % tiles_per_seq, 0)),
        out_shape=jax.ShapeDtypeStruct((batch, 2, seq, bw), BF16),
        compiler_params=_params(("parallel",), nb),
        name="fourier_channels",
    )(hin, chan_tbl)
    proj = proj.reshape(batch, 2 * seq, bw)
    outs = [matmul(pos_tbl, proj, out_dtype=BF16, tm=512, tn=512, w_lead=b) for b in range(batch)]
    return jnp.concatenate(outs, axis=0)


def _gmlp_kernel(u_ref, v_ref, lg_ref, lb_ref, ws_ref, bias_ref, o_ref, *, ts):
    v = v_ref[...]
    mu = jnp.mean(v, axis=-1, keepdims=True)
    vc = v - mu
    var = jnp.mean(vc * vc, axis=-1, keepdims=True)
    vn = (vc * lax.rsqrt(var + EPS) * lg_ref[...] + lb_ref[...]).astype(BF16)
    n_chunks = ts // GMLP_CHUNK
    n_heads = v.shape[1] // GMLP_HEAD_DIM
    for h in range(n_heads):
        cols = slice(h * GMLP_HEAD_DIM, (h + 1) * GMLP_HEAD_DIM)
        rhs = jnp.concatenate([vn[c * GMLP_CHUNK:(c + 1) * GMLP_CHUNK, cols] for c in range(n_chunks)], axis=1)
        s = jnp.dot(ws_ref[h].astype(BF16), rhs, preferred_element_type=F32)
        for c in range(n_chunks):
            rows = slice(c * GMLP_CHUNK, (c + 1) * GMLP_CHUNK)
            sc = s[:, c * GMLP_HEAD_DIM:(c + 1) * GMLP_HEAD_DIM] + bias_ref[:, cols]
            o_ref[rows, cols] = (u_ref[rows, cols] * sc).astype(o_ref.dtype)


def gmlp_branch(hin, ln_g, ln_b, ws, bias, *, bw, col_u, col_v, ts=512):
    t = hin.shape[0]
    ts = _pick(t, ts)
    assert ts % GMLP_CHUNK == 0
    n_heads = bw // GMLP_HEAD_DIM
    bias_rows = jnp.repeat(bias.T, GMLP_HEAD_DIM, axis=1)
    nb = 4 * _nbytes((ts, bw), F32) + 2 * _nbytes((ts, bw), BF16) + 6 * _nbytes((ts, bw), F32)
    return pl.pallas_call(
        functools.partial(_gmlp_kernel, ts=ts),
        grid=(t // ts,),
        in_specs=[pl.BlockSpec((ts, bw), lambda i: (i, col_u)), pl.BlockSpec((ts, bw), lambda i: (i, col_v)),
                  pl.BlockSpec((1, bw), lambda i: (0, 0)), pl.BlockSpec((1, bw), lambda i: (0, 0)),
                  pl.BlockSpec((n_heads, GMLP_CHUNK, GMLP_CHUNK), lambda i: (0, 0, 0)),
                  pl.BlockSpec((GMLP_CHUNK, bw), lambda i: (0, 0))],
        out_specs=pl.BlockSpec((ts, bw), lambda i: (i, 0)),
        out_shape=jax.ShapeDtypeStruct((t, bw), BF16),
        compiler_params=_params(("parallel",), nb),
        name="gmlp_branch",
    )(hin, hin, ln_g.reshape(1, bw), ln_b.reshape(1, bw), ws, bias_rows)


def _route(idx, tm):
    a = idx.size
    n_tiles = a // tm + N_EXPERTS
    experts = jnp.arange(N_EXPERTS, dtype=jnp.int32)
    e_flat = idx.reshape(a)
    onehot = (e_flat[:, None] == experts[None, :]).astype(jnp.int32)
    csum = jnp.cumsum(onehot, axis=0)
    counts = csum[-1]
    rank = jnp.sum((csum - onehot) * onehot, axis=1)
    tiles_e = (counts + tm - 1) // tm
    tile_end = jnp.cumsum(tiles_e)
    tile_start = tile_end - tiles_e
    pos = jnp.sum(onehot * tile_start[None, :], axis=1) * tm + rank
    src_tok = jnp.zeros((n_tiles * tm,), jnp.int32).at[pos].set(jnp.arange(a, dtype=jnp.int32) // TOP_K)
    n_used = tile_end[-1]
    tile_ids = jnp.minimum(jnp.arange(n_tiles, dtype=jnp.int32), n_used - 1)
    tile_expert = jnp.sum((tile_ids[:, None] >= tile_end[None, :]).astype(jnp.int32), axis=1)
    first = jnp.concatenate([jnp.ones((1,), jnp.int32), (tile_expert[1:] != tile_expert[:-1]).astype(jnp.int32)])
    present = tiles_e > 0
    first_present = jnp.min(jnp.where(present, experts, N_EXPERTS))
    last_present = jnp.max(jnp.where(present, experts, -1))
    later = jnp.logical_and(present[None, :], experts[None, :] > experts[:, None])
    next_present = jnp.min(jnp.where(later, experts[None, :], N_EXPERTS), axis=1)
    next_present = jnp.where(next_present == N_EXPERTS, first_present, next_present)
    nxt = next_present[tile_expert]
    last_group = (tile_expert == last_present).astype(jnp.int32)
    rows_valid = counts[tile_expert] - (tile_ids - tile_start[tile_expert]) * tm
    full = (rows_valid > tm // 2).astype(jnp.int32)
    sched = (tile_expert, first, nxt, last_group, full, n_used.reshape(1).astype(jnp.int32))
    return pos, src_tok, sched


def _row_copy(src_hbm, row, dst_ref, r, sem):
    return pltpu.make_async_copy(src_hbm.at[pl.ds(row, 1), :], dst_ref.at[pl.ds(r, 1), :], sem)


ROW_DMA_UNROLL = 8


def _gather_rows_kernel(tok_ref, nrows_ref, src_hbm, o_ref, buf_ref, sem, *, tg, n_steps):
    i = pl.program_id(0)

    def start_step(step):
        slot = step % 2

        def issue(r, c):
            _row_copy(src_hbm, tok_ref[step * tg + r], buf_ref.at[slot], r, sem.at[slot]).start()
            return c

        lax.fori_loop(0, tg, issue, 0, unroll=ROW_DMA_UNROLL)

    def active(step):
        return step * tg < nrows_ref[0]

    @pl.when(i == 0)
    def _():
        start_step(i)

    @pl.when(jnp.logical_and(i + 1 < n_steps, active(i + 1)))
    def _():
        start_step(i + 1)

    @pl.when(active(i))
    def _():
        slot = i % 2

        def drain(r, c):
            _row_copy(src_hbm, 0, buf_ref.at[slot], r, sem.at[slot]).wait()
            return c

        lax.fori_loop(0, tg, drain, 0, unroll=ROW_DMA_UNROLL)
        o_ref[...] = buf_ref[slot].astype(o_ref.dtype)

    @pl.when(jnp.logical_not(active(i)))
    def _():
        o_ref[...] = jnp.zeros_like(o_ref)


def gather_rows(src, tok, n_rows, *, out_dtype, tg=256):
    p = tok.shape[0]
    d = src.shape[1]
    tg = _pick(p, tg)
    n_steps = p // tg
    nb = 2 * _nbytes((tg, d), F32) + 2 * _nbytes((tg, d), out_dtype) + _nbytes((tg, d), F32)
    return pl.pallas_call(
        functools.partial(_gather_rows_kernel, tg=tg, n_steps=n_steps),
        grid_spec=pltpu.PrefetchScalarGridSpec(
            num_scalar_prefetch=2,
            grid=(n_steps,),
            in_specs=[pl.BlockSpec(memory_space=pl.ANY)],
            out_specs=pl.BlockSpec((tg, d), lambda i, tok, nrows: (i, 0)),
            scratch_shapes=[pltpu.VMEM((2, tg, d), F32), pltpu.SemaphoreType.DMA((2,))],
        ),
        out_shape=jax.ShapeDtypeStruct((p, d), out_dtype),
        compiler_params=_params(("arbitrary",), nb),
        name="gather_rows",
    )(tok, n_rows, src)


def _moe_ffn_kernel(te_ref, first_ref, nxt_ref, lastg_ref, full_ref, nu_ref, x_ref, *refs, n_w, tn, nj, tm):
    w_hbm = refs[:n_w]
    o_ref = refs[n_w]
    stage = refs[n_w + 1:2 * n_w + 1]
    wbf = refs[2 * n_w + 1:3 * n_w + 1]
    sem = refs[3 * n_w + 1]
    j = pl.program_id(0)
    n = pl.program_id(1)
    used = n < nu_ref[0]

    def copies(e, jj):
        cols = pl.ds(pl.multiple_of(jj * tn, tn), tn)
        return [pltpu.make_async_copy(w_hbm[q].at[e, :, cols], stage[q], sem.at[q]) for q in range(n_w)]

    @pl.when(jnp.logical_and(used, first_ref[n] == 1))
    def _():
        @pl.when(jnp.logical_and(j == 0, n == 0))
        def _():
            for c in copies(te_ref[n], j):
                c.start()

        for c in copies(te_ref[n], j):
            c.wait()
        for q in range(n_w):
            wbf[q][...] = stage[q][...].astype(BF16)
        is_last = lastg_ref[n] == 1

        @pl.when(jnp.logical_not(is_last))
        def _():
            for c in copies(nxt_ref[n], j):
                c.start()

        @pl.when(jnp.logical_and(is_last, j + 1 < nj))
        def _():
            for c in copies(nxt_ref[n], j + 1):
                c.start()

    def compute(rows):
        x = x_ref[0:rows, :]
        acc = jnp.dot(x, wbf[0][...], preferred_element_type=F32)
        if n_w == 2:
            acc = jax.nn.silu(acc) * jnp.dot(x, wbf[1][...], preferred_element_type=F32)
        o_ref[0:rows, :] = acc.astype(o_ref.dtype)

    is_full = full_ref[n] == 1

    @pl.when(jnp.logical_and(used, is_full))
    def _():
        compute(tm)

    @pl.when(jnp.logical_and(used, jnp.logical_not(is_full)))
    def _():
        compute(tm // 2)
        o_ref[tm // 2:tm, :] = jnp.zeros((tm - tm // 2, tn), o_ref.dtype)

    @pl.when(jnp.logical_not(used))
    def _():
        o_ref[...] = jnp.zeros_like(o_ref)


def moe_ffn(xs, sched, weights, *, out_dtype, tm, tn=512):
    p, k = xs.shape
    n_w = len(weights)
    n_out = weights[0].shape[2]
    tn = _pick(n_out, tn)
    nj = n_out // tn
    x_spec = pl.BlockSpec((tm, k), lambda j, n, te, fi, nx, lg, fu, nu: (jnp.minimum(n, nu[0] - 1), 0))
    o_spec = pl.BlockSpec((tm, tn), lambda j, n, te, fi, nx, lg, fu, nu: (n, j))
    nb = (2 * _nbytes((tm, k), BF16) + n_w * _nbytes((k, tn), F32) + n_w * _nbytes((k, tn), BF16)
          + 2 * _nbytes((tm, tn), out_dtype) + (n_w + 1) * _nbytes((tm, tn), F32))
    scratch = ([pltpu.VMEM((k, tn), F32)] * n_w + [pltpu.VMEM((k, tn), BF16)] * n_w
               + [pltpu.SemaphoreType.DMA((n_w,))])
    return pl.pallas_call(
        functools.partial(_moe_ffn_kernel, n_w=n_w, tn=tn, nj=nj, tm=tm),
        grid_spec=pltpu.PrefetchScalarGridSpec(
            num_scalar_prefetch=6,
            grid=(nj, p // tm),
            in_specs=[x_spec] + [pl.BlockSpec(memory_space=pl.ANY)] * n_w,
            out_specs=o_spec,
            scratch_shapes=scratch,
        ),
        out_shape=jax.ShapeDtypeStruct((p, n_out), out_dtype),
        compiler_params=_params(("arbitrary", "arbitrary"), nb),
        name="moe_up" if n_w == 2 else "moe_down",
    )(*sched, xs, *weights)


def _combine_norm_kernel(pos_ref, h_ref, wts_ref, ys_hbm, g_ref, o_ref, buf_ref, sem, *, tg, n_steps):
    i = pl.program_id(0)

    def start_step(step):
        slot = step % 2

        def issue(r, c):
            for k in range(TOP_K):
                row = pos_ref[(step * tg + r) * TOP_K + k]
                _row_copy(ys_hbm, row, buf_ref.at[slot, k], r, sem.at[slot]).start()
            return c

        lax.fori_loop(0, tg, issue, 0, unroll=ROW_DMA_UNROLL)

    @pl.when(i == 0)
    def _():
        start_step(i)

    @pl.when(i + 1 < n_steps)
    def _():
        start_step(i + 1)

    slot = i % 2

    def drain(r, c):
        for k in range(TOP_K):
            _row_copy(ys_hbm, 0, buf_ref.at[slot, k], r, sem.at[slot]).wait()
        return c

    lax.fori_loop(0, tg, drain, 0, unroll=ROW_DMA_UNROLL)
    h = h_ref[...]
    for k in range(TOP_K):
        h = h + wts_ref[:, k:k + 1] * buf_ref[slot, k]
    o_ref[...] = _rmsnorm_rows(h, g_ref[...]).astype(o_ref.dtype)


def combine_norm(h, wts, pos, ys, g, *, tg=128):
    t, d = h.shape
    tg = _pick(t, tg)
    n_steps = t // tg
    nb = 4 * _nbytes((tg, d), F32) + 2 * TOP_K * _nbytes((tg, d), F32) + 4 * _nbytes((tg, d), F32)
    return pl.pallas_call(
        functools.partial(_combine_norm_kernel, tg=tg, n_steps=n_steps),
        grid_spec=pltpu.PrefetchScalarGridSpec(
            num_scalar_prefetch=1,
            grid=(n_steps,),
            in_specs=[pl.BlockSpec((tg, d), lambda i, pos: (i, 0)), pl.BlockSpec((tg, TOP_K), lambda i, pos: (i, 0)),
                      pl.BlockSpec(memory_space=pl.ANY), pl.BlockSpec((1, d), lambda i, pos: (0, 0))],
            out_specs=pl.BlockSpec((tg, d), lambda i, pos: (i, 0)),
            scratch_shapes=[pltpu.VMEM((2, TOP_K, tg, d), F32), pltpu.SemaphoreType.DMA((2,))],
        ),
        out_shape=jax.ShapeDtypeStruct((t, d), F32),
        compiler_params=_params(("arbitrary",), nb),
        name="combine_norm",
    )(pos, h, wts, ys, g.reshape(1, d))


def moe_ffn_then_norm(h, norm_g, router, w1, w3, w2, final_g, *, tm=512):
    t = h.shape[0]
    tm = _pick(t * TOP_K, tm)
    hn, idx, wts = rmsnorm_router(h, norm_g, router)
    pos, src_tok, sched = _route(idx, tm)
    xs = gather_rows(hn, src_tok, sched[-1] * tm, out_dtype=BF16)
    acts = moe_ffn(xs, sched, (w1, w3), out_dtype=BF16, tm=tm)
    ys = moe_ffn(acts, sched, (w2,), out_dtype=F32, tm=tm)
    return combine_norm(h, wts, pos, ys, final_g)


def _mixer(h, norm1, w_in, w_gate, conv_w, conv_b, conv_ln_g, conv_ln_b, pool_w, pool_scale, gmlp_ln_g, gmlp_ln_b,
           gmlp_ws, gmlp_b, w_branch, w_out, tables, *, batch, seq):
    bw = w_branch.shape[1]
    xn = rmsnorm(h, norm1, BF16)
    hin = matmul(xn, w_in, out_dtype=F32)
    ya = conv_branch(hin, conv_w, conv_b, conv_ln_g, conv_ln_b, seq=seq, bw=bw)
    yb = pool_branch(hin, pool_w, pool_scale, seq=seq, bw=bw, col=2)
    yc = fourier_branch(hin, *tables, batch=batch, seq=seq, bw=bw, col=3)
    yd = gmlp_branch(hin, gmlp_ln_g, gmlp_ln_b, gmlp_ws, gmlp_b, bw=bw, col_u=4, col_v=5)
    gates = matmul(xn, w_gate, out_dtype=BF16, epilogue="sigmoid")
    merged = merge_branches((ya, yb, yc, yd), gates, w_branch)
    return matmul(merged, w_out, out_dtype=F32, res=h)


def kernel(x, l0_norm1, l0_w_in, l0_w_gate, l0_conv_w, l0_conv_b, l0_conv_ln_g, l0_conv_ln_b, l0_pool_w, l0_pool_scale, l0_gmlp_ln_g, l0_gmlp_ln_b, l0_gmlp_ws, l0_gmlp_b, l0_w_branch, l0_w_out, l0_norm2, l0_ffn_w1, l0_ffn_w3, l0_ffn_w2, l1_norm1, l1_w_in, l1_w_gate, l1_conv_w, l1_conv_b, l1_conv_ln_g, l1_conv_ln_b, l1_pool_w, l1_pool_scale, l1_gmlp_ln_g, l1_gmlp_ln_b, l1_gmlp_ws, l1_gmlp_b, l1_w_branch, l1_w_out, l1_norm2, l1_router, l1_exp_w1, l1_exp_w3, l1_exp_w2, final_norm):
    batch, seq, d = x.shape
    bw = l0_w_branch.shape[1]
    tables = _dft_tables(seq, bw // N_FFT_GROUPS)
    h = x.reshape(batch * seq, d)

    h = _mixer(h, l0_norm1, l0_w_in, l0_w_gate, l0_conv_w, l0_conv_b, l0_conv_ln_g, l0_conv_ln_b, l0_pool_w,
               l0_pool_scale, l0_gmlp_ln_g, l0_gmlp_ln_b, l0_gmlp_ws, l0_gmlp_b, l0_w_branch, l0_w_out, tables,
               batch=batch, seq=seq)
    hn = rmsnorm(h, l0_norm2, BF16)
    acts = swiglu_up(hn, l0_ffn_w1, l0_ffn_w3)
    half = l0_ffn_w2.shape[0] // 2
    h = matmul(acts, l0_ffn_w2, out_dtype=F32, res=h, tm=512, k_block=(0, half))
    h = matmul(acts, l0_ffn_w2, out_dtype=F32, res=h, tm=512, k_block=(1, half))

    h = _mixer(h, l1_norm1, l1_w_in, l1_w_gate, l1_conv_w, l1_conv_b, l1_conv_ln_g, l1_conv_ln_b, l1_pool_w,
               l1_pool_scale, l1_gmlp_ln_g, l1_gmlp_ln_b, l1_gmlp_ws, l1_gmlp_b, l1_w_branch, l1_w_out, tables,
               batch=batch, seq=seq)
    out = moe_ffn_then_norm(h, l1_norm2, l1_router, l1_exp_w1, l1_exp_w3, l1_exp_w2, final_norm)
    return out.reshape(batch, seq, d)
```

```python
import functools

import numpy as np
import jax
import jax.numpy as jnp
from jax import lax
from jax.experimental import pallas as pl
from jax.experimental.pallas import tpu as pltpu

F32 = jnp.float32
BF16 = jnp.bfloat16

N_BRANCH = 4
CONV_WIDTH = 31
CONV_PAD = CONV_WIDTH // 2
POOL_WINDOWS = (2, 4, 8, 16)
N_FFT_GROUPS = 4
GMLP_CHUNK = 128
GMLP_HEAD_DIM = 128
N_EXPERTS = 8
TOP_K = 2
EPS = 1e-6

HALO = 16
V7X_VMEM_BYTES = 64 * 1024 * 1024
VMEM_CAP = V7X_VMEM_BYTES - 6 * 1024 * 1024


def _vmem_limit(nbytes):
    return int(min(VMEM_CAP, nbytes + nbytes // 4 + (4 << 20)))


def _nbytes(shape, dtype):
    return int(np.prod(shape)) * jnp.dtype(dtype).itemsize


def _params(sem, nbytes):
    return pltpu.CompilerParams(dimension_semantics=sem, vmem_limit_bytes=_vmem_limit(nbytes))


def _pick(n, pref):
    t = min(n, pref)
    while n % t:
        t -= 8
    return t


def _rmsnorm_rows(x, g):
    return x * lax.rsqrt(jnp.mean(x * x, axis=-1, keepdims=True) + EPS) * g


def _rmsnorm_kernel(x_ref, g_ref, o_ref):
    o_ref[...] = _rmsnorm_rows(x_ref[...], g_ref[...]).astype(o_ref.dtype)


def rmsnorm(x, g, out_dtype):
    t, d = x.shape
    tm = _pick(t, 256)
    nb = 2 * _nbytes((tm, d), F32) + 2 * _nbytes((tm, d), out_dtype)
    return pl.pallas_call(
        _rmsnorm_kernel,
        grid=(t // tm,),
        in_specs=[pl.BlockSpec((tm, d), lambda i: (i, 0)), pl.BlockSpec((1, d), lambda i: (0, 0))],
        out_specs=pl.BlockSpec((tm, d), lambda i: (i, 0)),
        out_shape=jax.ShapeDtypeStruct((t, d), out_dtype),
        compiler_params=_params(("parallel",), nb),
        name="rmsnorm",
    )(x, g.reshape(1, d))


def _rmsnorm_router_kernel(x_ref, g_ref, r_ref, o_ref, idx_ref, wts_ref):
    y = _rmsnorm_rows(x_ref[...], g_ref[...])
    o_ref[...] = y
    n_e = r_ref.shape[1] // 2
    y_hi = y.astype(BF16)
    y_lo = (y - y_hi.astype(F32)).astype(BF16)
    p_hi = jnp.dot(y_hi, r_ref[...], preferred_element_type=F32)
    p_lo = jnp.dot(y_lo, r_ref[...], preferred_element_type=F32)
    logits = p_hi[:, :n_e] + (p_hi[:, n_e:] + p_lo[:, :n_e])
    lane = lax.broadcasted_iota(jnp.int32, logits.shape, 1)
    m1 = jnp.max(logits, axis=-1, keepdims=True)
    i1 = jnp.min(jnp.where(logits == m1, lane, n_e), axis=-1, keepdims=True)
    rest = jnp.where(lane == i1, -jnp.inf, logits)
    m2 = jnp.max(rest, axis=-1, keepdims=True)
    i2 = jnp.min(jnp.where(rest == m2, lane, n_e), axis=-1, keepdims=True)
    e2 = jnp.exp(m2 - m1)
    den = 1.0 + e2
    slot = lax.broadcasted_iota(jnp.int32, idx_ref.shape, 1)
    idx_ref[...] = jnp.where(slot == 0, i1, i2)
    wts_ref[...] = jnp.where(slot == 0, 1.0 / den, e2 / den)


def rmsnorm_router(x, g, router):
    t, d = x.shape
    n_e = router.shape[1]
    tm = _pick(t, 256)
    r_hi = router.astype(BF16)
    r_lo = (router - r_hi.astype(F32)).astype(BF16)
    r_parts = jnp.concatenate([r_hi, r_lo], axis=1)
    nb = 6 * _nbytes((tm, d), F32) + 2 * _nbytes((d, 128), BF16)
    return pl.pallas_call(
        _rmsnorm_router_kernel,
        grid=(t // tm,),
        in_specs=[pl.BlockSpec((tm, d), lambda i: (i, 0)), pl.BlockSpec((1, d), lambda i: (0, 0)),
                  pl.BlockSpec((d, 2 * n_e), lambda i: (0, 0))],
        out_specs=[pl.BlockSpec((tm, d), lambda i: (i, 0)), pl.BlockSpec((tm, TOP_K), lambda i: (i, 0)),
                   pl.BlockSpec((tm, TOP_K), lambda i: (i, 0))],
        out_shape=[jax.ShapeDtypeStruct((t, d), F32), jax.ShapeDtypeStruct((t, TOP_K), jnp.int32),
                   jax.ShapeDtypeStruct((t, TOP_K), F32)],
        compiler_params=_params(("parallel",), nb),
        name="rmsnorm_router",
    )(x, g.reshape(1, d), r_parts)


def _matmul_kernel(*refs, staged, has_res, epilogue, tk, tn, nj, k0):
    x_ref, w_ref = refs[0], refs[1]
    pos = 2
    r_ref = None
    if has_res:
        r_ref = refs[pos]
        pos += 1
    o_ref = refs[pos]
    if staged:
        stage_ref, wbf_ref, sem = refs[pos + 1:pos + 4]
        j = pl.program_id(0)

        def copy(jj):
            cols = pl.ds(pl.multiple_of(jj * tn, tn), tn)
            return pltpu.make_async_copy(w_ref.at[pl.ds(k0, tk), cols], stage_ref, sem)

        @pl.when(pl.program_id(1) == 0)
        def _():
            @pl.when(j == 0)
            def _():
                copy(j).start()

            copy(j).wait()
            wbf_ref[...] = stage_ref[...].astype(BF16)

            @pl.when(j + 1 < nj)
            def _():
                copy(j + 1).start()

        w = wbf_ref[...]
    else:
        w = w_ref[...]
    acc = jnp.dot(x_ref[...], w, preferred_element_type=F32)
    if epilogue == "sigmoid":
        acc = jax.nn.sigmoid(acc)
    if has_res:
        acc = acc + r_ref[...]
    o_ref[...] = acc.astype(o_ref.dtype)


def matmul(x, w, *, out_dtype, res=None, epilogue=None, tm=1024, tn=1024, k_block=None, w_lead=None):
    m, kx = x.shape
    kw, n = w.shape[-2], w.shape[-1]
    if k_block is None:
        assert kx == kw
        kk, tk = 0, kx
    else:
        kk, tk = k_block
    tm = _pick(m, tm)
    tn = _pick(n, tn)
    nj = n // tn
    staged = w.dtype != BF16
    nb = 2 * _nbytes((tm, tk), BF16) + 2 * _nbytes((tm, tn), out_dtype) + 2 * _nbytes((tm, tn), F32)
    scratch = []
    if staged:
        assert w.ndim == 2
        w_spec = pl.BlockSpec(memory_space=pl.ANY)
        scratch = [pltpu.VMEM((tk, tn), F32), pltpu.VMEM((tk, tn), BF16), pltpu.SemaphoreType.DMA(())]
        nb += _nbytes((tk, tn), F32) + _nbytes((tk, tn), BF16)
    elif w.ndim == 3:
        w_spec = pl.BlockSpec((None, tk, tn), lambda j, i: (w_lead, kk, j))
        nb += 2 * _nbytes((tk, tn), BF16)
    else:
        w_spec = pl.BlockSpec((tk, tn), lambda j, i: (kk, j))
        nb += 2 * _nbytes((tk, tn), BF16)
    in_specs = [pl.BlockSpec((tm, tk), lambda j, i: (i, kk)), w_spec]
    args = [x, w]
    if res is not None:
        in_specs.append(pl.BlockSpec((tm, tn), lambda j, i: (i, j)))
        args.append(res)
        nb += 2 * _nbytes((tm, tn), res.dtype)
    return pl.pallas_call(
        functools.partial(_matmul_kernel, staged=staged, has_res=res is not None, epilogue=epilogue,
                          tk=tk, tn=tn, nj=nj, k0=kk * tk),
        grid=(nj, m // tm),
        in_specs=in_specs,
        out_specs=pl.BlockSpec((tm, tn), lambda j, i: (i, j)),
        out_shape=jax.ShapeDtypeStruct((m, n), out_dtype),
        scratch_shapes=scratch,
        compiler_params=_params(("arbitrary", "arbitrary"), nb),
        name="matmul",
    )(*args)


def swiglu_up(x, w1, w3, *, tm=2048, tn=256):
    tm = _pick(x.shape[0], tm)
    n_tiles = x.shape[0] // tm
    zeros = jnp.zeros((n_tiles,), jnp.int32)
    ones = jnp.ones((n_tiles,), jnp.int32)
    first = zeros.at[0].set(1)
    sched = (zeros, first, zeros, ones, ones, jnp.full((1,), n_tiles, jnp.int32))
    return moe_ffn(x, sched, (w1[None], w3[None]), out_dtype=BF16, tm=tm, tn=tn, name="swiglu_up")


def _merge_kernel(ya_ref, yb_ref, yc_ref, yd_ref, g0_ref, g1_ref, g2_ref, g3_ref, wb_ref, o_ref, wbf_ref):
    @pl.when(pl.program_id(1) == 0)
    def _():
        wbf_ref[...] = wb_ref[...].astype(BF16)

    acc = None
    for g, (y_ref, g_ref) in enumerate(((ya_ref, g0_ref), (yb_ref, g1_ref), (yc_ref, g2_ref), (yd_ref, g3_ref))):
        proj = jnp.dot(y_ref[...], wbf_ref[g], preferred_element_type=F32)
        term = g_ref[...].astype(F32) * proj
        acc = term if acc is None else acc + term
    o_ref[...] = acc.astype(o_ref.dtype)


def merge_branches(branches, gates, w_branch, *, tm=512, tn=512):
    t, bw = branches[0].shape
    d = w_branch.shape[2]
    tm = _pick(t, tm)
    tn = _pick(d, tn)
    nj = d // tn
    y_spec = pl.BlockSpec((tm, bw), lambda j, i: (i, 0))
    gate_specs = [pl.BlockSpec((tm, tn), functools.partial(lambda j, i, g: (i, g * nj + j), g=g))
                  for g in range(N_BRANCH)]
    nb = (2 * N_BRANCH * _nbytes((tm, bw), BF16) + 2 * N_BRANCH * _nbytes((tm, tn), BF16)
          + 2 * _nbytes((N_BRANCH, bw, tn), F32) + _nbytes((N_BRANCH, bw, tn), BF16)
          + 2 * _nbytes((tm, tn), BF16) + 3 * _nbytes((tm, tn), F32))
    return pl.pallas_call(
        _merge_kernel,
        grid=(nj, t // tm),
        in_specs=[y_spec] * N_BRANCH + gate_specs + [pl.BlockSpec((N_BRANCH, bw, tn), lambda j, i: (0, 0, j))],
        out_specs=pl.BlockSpec((tm, tn), lambda j, i: (i, j)),
        out_shape=jax.ShapeDtypeStruct((t, d), BF16),
        scratch_shapes=[pltpu.VMEM((N_BRANCH, bw, tn), BF16)],
        compiler_params=_params(("arbitrary", "arbitrary"), nb),
        name="merge_branches",
    )(*branches, gates, gates, gates, gates, w_branch)


def _halo_specs(ts, width, col, seq_tiles_total):
    r = ts // HALO
    last = seq_tiles_total * r - 1
    prev = pl.BlockSpec((HALO, width), lambda i: (jnp.maximum(i * r - 1, 0), col))
    cur = pl.BlockSpec((ts, width), lambda i: (i, col))
    nxt = pl.BlockSpec((HALO, width), lambda i: (jnp.minimum((i + 1) * r, last), col))
    return [prev, cur, nxt]


def _edge_flags(ts, seq):
    tiles_per_seq = seq // ts
    k = pl.program_id(0) % tiles_per_seq
    return k == 0, k == tiles_per_seq - 1, k * ts


SUBLANES = 8


def _conv_kernel(lp_ref, lc_ref, ln_ref, gp_ref, gc_ref, gn_ref, cw_ref, cb_ref, lg_ref, lb_ref, o_ref,
                 abuf_ref, ash_ref, ybuf_ref, *, ts, seq):
    first, last, _ = _edge_flags(ts, seq)
    width = lc_ref.shape[1]
    glu = lambda l_ref, g_ref: l_ref[...].astype(F32) * jax.nn.sigmoid(g_ref[...].astype(F32))
    abuf_ref[0:HALO, :] = jnp.where(first, 0.0, glu(lp_ref, gp_ref))
    abuf_ref[HALO + ts:2 * HALO + ts, :] = jnp.where(last, 0.0, glu(ln_ref, gn_ref))
    abuf_ref[HALO:HALO + ts, :] = glu(lc_ref, gc_ref)

    span = ash_ref.shape[1]
    for p in range(1, SUBLANES):
        ash_ref[p - 1] = abuf_ref[p:p + span, :]

    rows = 64 if ts % 64 == 0 else ts
    for r0 in range(0, ts, rows):
        for c0 in range(0, width, 128):
            acc = jnp.zeros((rows, 128), F32)
            for k in range(CONV_WIDTH):
                q, p = divmod(HALO - CONV_PAD + k, SUBLANES)
                start = r0 + q * SUBLANES
                src = abuf_ref if p == 0 else ash_ref.at[p - 1]
                acc = acc + cw_ref[k:k + 1, c0:c0 + 128] * src[start:start + rows, c0:c0 + 128]
            ybuf_ref[r0:r0 + rows, c0:c0 + 128] = acc + cb_ref[:, c0:c0 + 128]

    y = ybuf_ref[...]
    mu = jnp.mean(y, axis=-1, keepdims=True)
    yc = y - mu
    var = jnp.mean(yc * yc, axis=-1, keepdims=True)
    z = yc * lax.rsqrt(var + EPS) * lg_ref[...] + lb_ref[...]
    o_ref[...] = jax.nn.silu(z).astype(o_ref.dtype)


def conv_branch(hin, conv_w, conv_b, ln_g, ln_b, *, seq, bw, ts=256):
    t = hin.shape[0]
    ts = _pick(seq, ts)
    n_tiles = t // ts
    row = lambda v: v.reshape(1, bw)
    const = lambda shape: pl.BlockSpec(shape, lambda i: (0, 0))
    span = ts + 2 * HALO - SUBLANES
    nb = (4 * _nbytes((ts + 2 * HALO, bw), BF16) + 2 * _nbytes((ts, bw), BF16) + _nbytes((ts + 2 * HALO, bw), F32)
          + (SUBLANES - 1) * _nbytes((span, bw), F32) + 5 * _nbytes((ts, bw), F32))
    return pl.pallas_call(
        functools.partial(_conv_kernel, ts=ts, seq=seq),
        grid=(n_tiles,),
        in_specs=_halo_specs(ts, bw, 0, n_tiles) + _halo_specs(ts, bw, 1, n_tiles)
        + [const((CONV_WIDTH, bw)), const((1, bw)), const((1, bw)), const((1, bw))],
        out_specs=pl.BlockSpec((ts, bw), lambda i: (i, 0)),
        out_shape=jax.ShapeDtypeStruct((t, bw), BF16),
        scratch_shapes=[pltpu.VMEM((ts + 2 * HALO, bw), F32), pltpu.VMEM((SUBLANES - 1, span, bw), F32),
                        pltpu.VMEM((ts, bw), F32)],
        compiler_params=_params(("parallel",), nb),
        name="conv_branch",
    )(hin, hin, hin, hin, hin, hin, conv_w, row(conv_b), row(ln_g), row(ln_b))


def _pool_kernel(pp_ref, pc_ref, pn_ref, pw_ref, ps_ref, o_ref, pbuf_ref, *, ts, seq):
    first, last, pos0 = _edge_flags(ts, seq)
    n_g = len(POOL_WINDOWS)
    pg = pc_ref.shape[1] // n_g
    pbuf_ref[0:HALO, :] = jnp.where(first, 0.0, pp_ref[...].astype(F32))
    pbuf_ref[HALO + ts:2 * HALO + ts, :] = jnp.where(last, 0.0, pn_ref[...].astype(F32))
    pbuf_ref[HALO:HALO + ts, :] = pc_ref[...].astype(F32)
    pos = pos0 + lax.broadcasted_iota(jnp.int32, (ts, 1), 0)
    for g, w in enumerate(POOL_WINDOWS):
        cols = slice(g * pg, (g + 1) * pg)
        win = None
        for d in range(-(w // 2), w // 2):
            term = pbuf_ref[HALO + d:HALO + d + ts, cols]
            win = term if win is None else win + term
        lo = jnp.maximum(pos - w // 2, 0)
        hi = jnp.minimum(pos + w // 2 - 1, seq - 1)
        cnt = (hi - lo + 1).astype(F32)
        pooled = win / cnt - pbuf_ref[HALO:HALO + ts, cols]
        mixed = jnp.dot(pooled.astype(BF16), pw_ref[g].astype(BF16), preferred_element_type=F32)
        o_ref[:, cols] = (mixed * ps_ref[:, cols]).astype(o_ref.dtype)


def pool_branch(hin, pool_w, pool_scale, *, seq, bw, col, ts=256):
    t = hin.shape[0]
    ts = _pick(seq, ts)
    n_tiles = t // ts
    nb = 2 * _nbytes((ts + 2 * HALO, bw), F32) + 2 * _nbytes((ts, bw), BF16) + 4 * _nbytes((ts + 2 * HALO, bw), F32)
    nb += 2 * _nbytes(pool_w.shape, F32)
    return pl.pallas_call(
        functools.partial(_pool_kernel, ts=ts, seq=seq),
        grid=(n_tiles,),
        in_specs=_halo_specs(ts, bw, col, n_tiles)
        + [pl.BlockSpec(pool_w.shape, lambda i: (0, 0, 0)), pl.BlockSpec((1, bw), lambda i: (0, 0))],
        out_specs=pl.BlockSpec((ts, bw), lambda i: (i, 0)),
        out_shape=jax.ShapeDtypeStruct((t, bw), BF16),
        scratch_shapes=[pltpu.VMEM((ts + 2 * HALO, bw), F32)],
        compiler_params=_params(("parallel",), nb),
        name="pool_branch",
    )(hin, hin, hin, pool_w, pool_scale.reshape(1, bw))


def _dft_tables(seq, group):
    c = np.arange(group)
    ang = 2.0 * np.pi * np.outer(c, c) / group
    scale = 1.0 / np.sqrt(float(seq) * group)
    chan = np.concatenate([np.cos(ang), np.sin(ang)], axis=1) * scale
    fa = 64 if seq % 64 == 0 else 1
    sp = np.arange(seq)[:, None]
    ang_a = 2.0 * np.pi * ((sp * fa * np.arange(seq // fa)[None, :]) % seq) / seq
    ang_b = 2.0 * np.pi * ((sp * np.arange(fa)[None, :]) % seq) / seq
    ca, sa = jnp.asarray(np.cos(ang_a), F32)[:, :, None], jnp.asarray(np.sin(ang_a), F32)[:, :, None]
    cb, sb = jnp.asarray(np.cos(ang_b), F32)[:, None, :], jnp.asarray(np.sin(ang_b), F32)[:, None, :]
    cos_s = (ca * cb - sa * sb).reshape(seq, seq)
    sin_s = (sa * cb + ca * sb).reshape(seq, seq)
    pos = jnp.concatenate([cos_s, -sin_s], axis=1).astype(BF16)
    return jnp.asarray(chan, F32).astype(BF16), pos


def _chan_dft_kernel(f_ref, tbl_ref, o_ref):
    n_g = N_FFT_GROUPS
    fg = f_ref.shape[1] // n_g
    tbl = tbl_ref[...]
    for g in range(n_g):
        r = jnp.dot(f_ref[:, g * fg:(g + 1) * fg].astype(BF16), tbl, preferred_element_type=F32)
        o_ref[0, :, g * fg:(g + 1) * fg] = r[:, :fg].astype(o_ref.dtype)
        o_ref[1, :, g * fg:(g + 1) * fg] = r[:, fg:].astype(o_ref.dtype)


def fourier_branch(hin, chan_tbl, pos_tbl, *, batch, seq, bw, col, ts=512):
    t = hin.shape[0]
    ts = _pick(seq, ts)
    tiles_per_seq = seq // ts
    nb = 2 * _nbytes((ts, bw), F32) + 4 * _nbytes((ts, bw), BF16) + 2 * _nbytes(chan_tbl.shape, BF16)
    nb += 4 * _nbytes((ts, bw), F32)
    proj = pl.pallas_call(
        _chan_dft_kernel,
        grid=(t // ts,),
        in_specs=[pl.BlockSpec((ts, bw), lambda i: (i, col)), pl.BlockSpec(chan_tbl.shape, lambda i: (0, 0))],
        out_specs=pl.BlockSpec((None, 2, ts, bw), lambda i: (i // tiles_per_seq, 0, i % tiles_per_seq, 0)),
        out_shape=jax.ShapeDtypeStruct((batch, 2, seq, bw), BF16),
        compiler_params=_params(("parallel",), nb),
        name="fourier_channels",
    )(hin, chan_tbl)
    proj = proj.reshape(batch, 2 * seq, bw)
    outs = [matmul(pos_tbl, proj, out_dtype=BF16, tm=512, tn=512, w_lead=b) for b in range(batch)]
    return jnp.concatenate(outs, axis=0)


def _gmlp_kernel(u_ref, v_ref, lg_ref, lb_ref, ws_ref, bias_ref, o_ref, *, ts):
    v = v_ref[...].astype(F32)
    mu = jnp.mean(v, axis=-1, keepdims=True)
    vc = v - mu
    var = jnp.mean(vc * vc, axis=-1, keepdims=True)
    vn = (vc * lax.rsqrt(var + EPS) * lg_ref[...] + lb_ref[...]).astype(BF16)
    n_chunks = ts // GMLP_CHUNK
    n_heads = v.shape[1] // GMLP_HEAD_DIM
    for h in range(n_heads):
        cols = slice(h * GMLP_HEAD_DIM, (h + 1) * GMLP_HEAD_DIM)
        rhs = jnp.concatenate([vn[c * GMLP_CHUNK:(c + 1) * GMLP_CHUNK, cols] for c in range(n_chunks)], axis=1)
        s = jnp.dot(ws_ref[h].astype(BF16), rhs, preferred_element_type=F32)
        for c in range(n_chunks):
            rows = slice(c * GMLP_CHUNK, (c + 1) * GMLP_CHUNK)
            sc = s[:, c * GMLP_HEAD_DIM:(c + 1) * GMLP_HEAD_DIM] + bias_ref[:, cols]
            o_ref[rows, cols] = (u_ref[rows, cols].astype(F32) * sc).astype(o_ref.dtype)


def gmlp_branch(hin, ln_g, ln_b, ws, bias, *, bw, col_u, col_v, ts=512):
    t = hin.shape[0]
    ts = _pick(t, ts)
    assert ts % GMLP_CHUNK == 0
    n_heads = bw // GMLP_HEAD_DIM
    bias_rows = jnp.repeat(bias.T, GMLP_HEAD_DIM, axis=1)
    nb = 4 * _nbytes((ts, bw), F32) + 2 * _nbytes((ts, bw), BF16) + 6 * _nbytes((ts, bw), F32)
    return pl.pallas_call(
        functools.partial(_gmlp_kernel, ts=ts),
        grid=(t // ts,),
        in_specs=[pl.BlockSpec((ts, bw), lambda i: (i, col_u)), pl.BlockSpec((ts, bw), lambda i: (i, col_v)),
                  pl.BlockSpec((1, bw), lambda i: (0, 0)), pl.BlockSpec((1, bw), lambda i: (0, 0)),
                  pl.BlockSpec((n_heads, GMLP_CHUNK, GMLP_CHUNK), lambda i: (0, 0, 0)),
                  pl.BlockSpec((GMLP_CHUNK, bw), lambda i: (0, 0))],
        out_specs=pl.BlockSpec((ts, bw), lambda i: (i, 0)),
        out_shape=jax.ShapeDtypeStruct((t, bw), BF16),
        compiler_params=_params(("parallel",), nb),
        name="gmlp_branch",
    )(hin, hin, ln_g.reshape(1, bw), ln_b.reshape(1, bw), ws, bias_rows)


def _route(idx, tm):
    a = idx.size
    n_tiles = a // tm + N_EXPERTS
    experts = jnp.arange(N_EXPERTS, dtype=jnp.int32)
    e_flat = idx.reshape(a)
    onehot = (e_flat[:, None] == experts[None, :]).astype(jnp.int32)
    csum = jnp.cumsum(onehot, axis=0)
    counts = csum[-1]
    rank = jnp.sum((csum - onehot) * onehot, axis=1)
    tiles_e = (counts + tm - 1) // tm
    tile_end = jnp.cumsum(tiles_e)
    tile_start = tile_end - tiles_e
    pos = jnp.sum(onehot * tile_start[None, :], axis=1) * tm + rank
    src_tok = jnp.zeros((n_tiles * tm,), jnp.int32).at[pos].set(jnp.arange(a, dtype=jnp.int32) // TOP_K)
    n_used = tile_end[-1]
    tile_ids = jnp.minimum(jnp.arange(n_tiles, dtype=jnp.int32), n_used - 1)
    tile_expert = jnp.sum((tile_ids[:, None] >= tile_end[None, :]).astype(jnp.int32), axis=1)
    first = jnp.concatenate([jnp.ones((1,), jnp.int32), (tile_expert[1:] != tile_expert[:-1]).astype(jnp.int32)])
    present = tiles_e > 0
    first_present = jnp.min(jnp.where(present, experts, N_EXPERTS))
    last_present = jnp.max(jnp.where(present, experts, -1))
    later = jnp.logical_and(present[None, :], experts[None, :] > experts[:, None])
    next_present = jnp.min(jnp.where(later, experts[None, :], N_EXPERTS), axis=1)
    next_present = jnp.where(next_present == N_EXPERTS, first_present, next_present)
    nxt = next_present[tile_expert]
    last_group = (tile_expert == last_present).astype(jnp.int32)
    rows_valid = counts[tile_expert] - (tile_ids - tile_start[tile_expert]) * tm
    full = (rows_valid > tm // 2).astype(jnp.int32)
    sched = (tile_expert, first, nxt, last_group, full, n_used.reshape(1).astype(jnp.int32))
    return pos, src_tok, sched


def _row_copy(src_hbm, row, dst_ref, r, sem):
    return pltpu.make_async_copy(src_hbm.at[pl.ds(row, 1), :], dst_ref.at[pl.ds(r, 1), :], sem)


ROW_DMA_UNROLL = 8
ROW_ISSUE_STRIDE = 37


def _gather_rows_kernel(tok_ref, nrows_ref, src_hbm, o_ref, buf_ref, sem, *, tg, n_steps):
    i = pl.program_id(0)

    def start_step(step):
        slot = step % 2

        def issue(r, c):
            rr = lax.rem(r * ROW_ISSUE_STRIDE, tg)
            _row_copy(src_hbm, tok_ref[step * tg + rr], buf_ref.at[slot], rr, sem.at[slot]).start()
            return c

        lax.fori_loop(0, tg, issue, 0, unroll=ROW_DMA_UNROLL)

    def active(step):
        return step * tg < nrows_ref[0]

    @pl.when(i == 0)
    def _():
        start_step(i)

    @pl.when(jnp.logical_and(i + 1 < n_steps, active(i + 1)))
    def _():
        start_step(i + 1)

    @pl.when(active(i))
    def _():
        slot = i % 2

        def drain(r, c):
            _row_copy(src_hbm, 0, buf_ref.at[slot], r, sem.at[slot]).wait()
            return c

        lax.fori_loop(0, tg, drain, 0, unroll=ROW_DMA_UNROLL)
        o_ref[...] = buf_ref[slot].astype(o_ref.dtype)

    @pl.when(jnp.logical_not(active(i)))
    def _():
        o_ref[...] = jnp.zeros_like(o_ref)


def gather_rows(src, tok, n_rows, *, out_dtype, tg=256):
    p = tok.shape[0]
    d = src.shape[1]
    tg = _pick(p, tg)
    assert tg % ROW_ISSUE_STRIDE
    n_steps = p // tg
    nb = 2 * _nbytes((tg, d), F32) + 2 * _nbytes((tg, d), out_dtype) + _nbytes((tg, d), F32)
    return pl.pallas_call(
        functools.partial(_gather_rows_kernel, tg=tg, n_steps=n_steps),
        grid_spec=pltpu.PrefetchScalarGridSpec(
            num_scalar_prefetch=2,
            grid=(n_steps,),
            in_specs=[pl.BlockSpec(memory_space=pl.ANY)],
            out_specs=pl.BlockSpec((tg, d), lambda i, tok, nrows: (i, 0)),
            scratch_shapes=[pltpu.VMEM((2, tg, d), F32), pltpu.SemaphoreType.DMA((2,))],
        ),
        out_shape=jax.ShapeDtypeStruct((p, d), out_dtype),
        compiler_params=_params(("arbitrary",), nb),
        name="gather_rows",
    )(tok, n_rows, src)


def _moe_ffn_kernel(te_ref, first_ref, nxt_ref, lastg_ref, full_ref, nu_ref, x_ref, *refs, n_w, tn, nj, tm):
    w_hbm = refs[:n_w]
    o_ref = refs[n_w]
    stage = refs[n_w + 1:2 * n_w + 1]
    wbf = refs[2 * n_w + 1:3 * n_w + 1]
    sem = refs[3 * n_w + 1]
    j = pl.program_id(0)
    n = pl.program_id(1)
    used = n < nu_ref[0]

    def copies(e, jj):
        cols = pl.ds(pl.multiple_of(jj * tn, tn), tn)
        return [pltpu.make_async_copy(w_hbm[q].at[e, :, cols], stage[q], sem.at[q]) for q in range(n_w)]

    @pl.when(jnp.logical_and(used, first_ref[n] == 1))
    def _():
        @pl.when(jnp.logical_and(j == 0, n == 0))
        def _():
            for c in copies(te_ref[n], j):
                c.start()

        for c in copies(te_ref[n], j):
            c.wait()
        for q in range(n_w):
            wbf[q][...] = stage[q][...].astype(BF16)
        is_last = lastg_ref[n] == 1

        @pl.when(jnp.logical_not(is_last))
        def _():
            for c in copies(nxt_ref[n], j):
                c.start()

        @pl.when(jnp.logical_and(is_last, j + 1 < nj))
        def _():
            for c in copies(nxt_ref[n], j + 1):
                c.start()

    def compute(rows):
        x = x_ref[0:rows, :]
        acc = jnp.dot(x, wbf[0][...], preferred_element_type=F32)
        if n_w == 2:
            acc = jax.nn.silu(acc) * jnp.dot(x, wbf[1][...], preferred_element_type=F32)
        o_ref[0:rows, :] = acc.astype(o_ref.dtype)

    is_full = full_ref[n] == 1

    @pl.when(jnp.logical_and(used, is_full))
    def _():
        compute(tm)

    @pl.when(jnp.logical_and(used, jnp.logical_not(is_full)))
    def _():
        compute(tm // 2)
        o_ref[tm // 2:tm, :] = jnp.zeros((tm - tm // 2, tn), o_ref.dtype)

    @pl.when(jnp.logical_not(used))
    def _():
        o_ref[...] = jnp.zeros_like(o_ref)


def moe_ffn(xs, sched, weights, *, out_dtype, tm, tn=512, name="moe_ffn"):
    p, k = xs.shape
    n_w = len(weights)
    n_out = weights[0].shape[2]
    tn = _pick(n_out, tn)
    nj = n_out // tn
    x_spec = pl.BlockSpec((tm, k), lambda j, n, te, fi, nx, lg, fu, nu: (jnp.minimum(n, nu[0] - 1), 0))
    o_spec = pl.BlockSpec((tm, tn), lambda j, n, te, fi, nx, lg, fu, nu: (n, j))
    nb = (2 * _nbytes((tm, k), BF16) + n_w * _nbytes((k, tn), F32) + n_w * _nbytes((k, tn), BF16)
          + 2 * _nbytes((tm, tn), out_dtype) + (n_w + 1) * _nbytes((tm, tn), F32))
    scratch = ([pltpu.VMEM((k, tn), F32)] * n_w + [pltpu.VMEM((k, tn), BF16)] * n_w
               + [pltpu.SemaphoreType.DMA((n_w,))])
    return pl.pallas_call(
        functools.partial(_moe_ffn_kernel, n_w=n_w, tn=tn, nj=nj, tm=tm),
        grid_spec=pltpu.PrefetchScalarGridSpec(
            num_scalar_prefetch=6,
            grid=(nj, p // tm),
            in_specs=[x_spec] + [pl.BlockSpec(memory_space=pl.ANY)] * n_w,
            out_specs=o_spec,
            scratch_shapes=scratch,
        ),
        out_shape=jax.ShapeDtypeStruct((p, n_out), out_dtype),
        compiler_params=_params(("arbitrary", "arbitrary"), nb),
        name=name,
    )(*sched, xs, *weights)


def _combine_norm_kernel(pos_ref, h_ref, wts_ref, ys_hbm, g_ref, o_ref, buf_ref, sem, *, tg, n_steps):
    i = pl.program_id(0)

    def start_step(step):
        slot = step % 2

        def issue(r, c):
            for k in range(TOP_K):
                row = pos_ref[(step * tg + r) * TOP_K + k]
                _row_copy(ys_hbm, row, buf_ref.at[slot, k], r, sem.at[slot]).start()
            return c

        lax.fori_loop(0, tg, issue, 0, unroll=ROW_DMA_UNROLL)

    @pl.when(i == 0)
    def _():
        start_step(i)

    @pl.when(i + 1 < n_steps)
    def _():
        start_step(i + 1)

    slot = i % 2

    def drain(r, c):
        for k in range(TOP_K):
            _row_copy(ys_hbm, 0, buf_ref.at[slot, k], r, sem.at[slot]).wait()
        return c

    lax.fori_loop(0, tg, drain, 0, unroll=ROW_DMA_UNROLL)
    h = h_ref[...]
    for k in range(TOP_K):
        h = h + wts_ref[:, k:k + 1] * buf_ref[slot, k]
    o_ref[...] = _rmsnorm_rows(h, g_ref[...]).astype(o_ref.dtype)


def combine_norm(h, wts, pos, ys, g, *, tg=128):
    t, d = h.shape
    tg = _pick(t, tg)
    n_steps = t // tg
    nb = 4 * _nbytes((tg, d), F32) + 2 * TOP_K * _nbytes((tg, d), F32) + 4 * _nbytes((tg, d), F32)
    return pl.pallas_call(
        functools.partial(_combine_norm_kernel, tg=tg, n_steps=n_steps),
        grid_spec=pltpu.PrefetchScalarGridSpec(
            num_scalar_prefetch=1,
            grid=(n_steps,),
            in_specs=[pl.BlockSpec((tg, d), lambda i, pos: (i, 0)), pl.BlockSpec((tg, TOP_K), lambda i, pos: (i, 0)),
                      pl.BlockSpec(memory_space=pl.ANY), pl.BlockSpec((1, d), lambda i, pos: (0, 0))],
            out_specs=pl.BlockSpec((tg, d), lambda i, pos: (i, 0)),
            scratch_shapes=[pltpu.VMEM((2, TOP_K, tg, d), F32), pltpu.SemaphoreType.DMA((2,))],
        ),
        out_shape=jax.ShapeDtypeStruct((t, d), F32),
        compiler_params=_params(("arbitrary",), nb),
        name="combine_norm",
    )(pos, h, wts, ys, g.reshape(1, d))


def moe_ffn_then_norm(h, norm_g, router, w1, w3, w2, final_g, *, tm=512):
    t = h.shape[0]
    tm = _pick(t * TOP_K, tm)
    hn, idx, wts = rmsnorm_router(h, norm_g, router)
    pos, src_tok, sched = _route(idx, tm)
    xs = gather_rows(hn, src_tok, sched[-1] * tm, out_dtype=BF16)
    acts = moe_ffn(xs, sched, (w1, w3), out_dtype=BF16, tm=tm, name="moe_up")
    ys = moe_ffn(acts, sched, (w2,), out_dtype=F32, tm=tm, tn=1024, name="moe_down")
    return combine_norm(h, wts, pos, ys, final_g)


def _mixer(h, norm1, w_in, w_gate, conv_w, conv_b, conv_ln_g, conv_ln_b, pool_w, pool_scale, gmlp_ln_g, gmlp_ln_b,
           gmlp_ws, gmlp_b, w_branch, w_out, tables, *, batch, seq):
    bw = w_branch.shape[1]
    xn = rmsnorm(h, norm1, BF16)
    hin = matmul(xn, w_in, out_dtype=BF16)
    ya = conv_branch(hin, conv_w, conv_b, conv_ln_g, conv_ln_b, seq=seq, bw=bw)
    yb = pool_branch(hin, pool_w, pool_scale, seq=seq, bw=bw, col=2)
    yc = fourier_branch(hin, *tables, batch=batch, seq=seq, bw=bw, col=3)
    yd = gmlp_branch(hin, gmlp_ln_g, gmlp_ln_b, gmlp_ws, gmlp_b, bw=bw, col_u=4, col_v=5)
    gates = matmul(xn, w_gate, out_dtype=BF16, epilogue="sigmoid")
    merged = merge_branches((ya, yb, yc, yd), gates, w_branch)
    return matmul(merged, w_out, out_dtype=F32, res=h, tm=512)


def kernel(x, l0_norm1, l0_w_in, l0_w_gate, l0_conv_w, l0_conv_b, l0_conv_ln_g, l0_conv_ln_b, l0_pool_w, l0_pool_scale, l0_gmlp_ln_g, l0_gmlp_ln_b, l0_gmlp_ws, l0_gmlp_b, l0_w_branch, l0_w_out, l0_norm2, l0_ffn_w1, l0_ffn_w3, l0_ffn_w2, l1_norm1, l1_w_in, l1_w_gate, l1_conv_w, l1_conv_b, l1_conv_ln_g, l1_conv_ln_b, l1_pool_w, l1_pool_scale, l1_gmlp_ln_g, l1_gmlp_ln_b, l1_gmlp_ws, l1_gmlp_b, l1_w_branch, l1_w_out, l1_norm2, l1_router, l1_exp_w1, l1_exp_w3, l1_exp_w2, final_norm):
    batch, seq, d = x.shape
    bw = l0_w_branch.shape[1]
    tables = _dft_tables(seq, bw // N_FFT_GROUPS)
    h = x.reshape(batch * seq, d)

    h = _mixer(h, l0_norm1, l0_w_in, l0_w_gate, l0_conv_w, l0_conv_b, l0_conv_ln_g, l0_conv_ln_b, l0_pool_w,
               l0_pool_scale, l0_gmlp_ln_g, l0_gmlp_ln_b, l0_gmlp_ws, l0_gmlp_b, l0_w_branch, l0_w_out, tables,
               batch=batch, seq=seq)
    hn = rmsnorm(h, l0_norm2, BF16)
    acts = swiglu_up(hn, l0_ffn_w1, l0_ffn_w3)
    half = l0_ffn_w2.shape[0] // 2
    h = matmul(acts, l0_ffn_w2, out_dtype=F32, res=h, tn=512, k_block=(0, half))
    h = matmul(acts, l0_ffn_w2, out_dtype=F32, res=h, tn=512, k_block=(1, half))

    h = _mixer(h, l1_norm1, l1_w_in, l1_w_gate, l1_conv_w, l1_conv_b, l1_conv_ln_g, l1_conv_ln_b, l1_pool_w,
               l1_pool_scale, l1_gmlp_ln_g, l1_gmlp_ln_b, l1_gmlp_ws, l1_gmlp_b, l1_w_branch, l1_w_out, tables,
               batch=batch, seq=seq)
    out = moe_ffn_then_norm(h, l1_norm2, l1_router, l1_exp_w1, l1_exp_w3, l1_exp_w2, final_norm)
    return out.reshape(batch, seq, d)
```

```python
import functools

import numpy as np
import jax
import jax.numpy as jnp
from jax import lax
from jax.experimental import pallas as pl
from jax.experimental.pallas import tpu as pltpu

F32 = jnp.float32
BF16 = jnp.bfloat16

N_BRANCH = 4
CONV_WIDTH = 31
CONV_PAD = CONV_WIDTH // 2
POOL_WINDOWS = (2, 4, 8, 16)
N_FFT_GROUPS = 4
GMLP_CHUNK = 128
GMLP_HEAD_DIM = 128
N_EXPERTS = 8
TOP_K = 2
EPS = 1e-6

HALO = 16
V7X_VMEM_BYTES = 64 * 1024 * 1024
VMEM_CAP = V7X_VMEM_BYTES - 6 * 1024 * 1024


def _vmem_limit(nbytes):
    return int(min(VMEM_CAP, nbytes + nbytes // 4 + (4 << 20)))


def _nbytes(shape, dtype):
    return int(np.prod(shape)) * jnp.dtype(dtype).itemsize


def _params(sem, nbytes):
    return pltpu.CompilerParams(dimension_semantics=sem, vmem_limit_bytes=_vmem_limit(nbytes))


def _pick(n, pref):
    t = min(n, pref)
    while n % t:
        t -= 8
    return t


def _rmsnorm_rows(x, g):
    return x * lax.rsqrt(jnp.mean(x * x, axis=-1, keepdims=True) + EPS) * g


def _rmsnorm_kernel(x_ref, g_ref, o_ref):
    o_ref[...] = _rmsnorm_rows(x_ref[...], g_ref[...]).astype(o_ref.dtype)


def rmsnorm(x, g, out_dtype):
    t, d = x.shape
    tm = _pick(t, 256)
    nb = 2 * _nbytes((tm, d), F32) + 2 * _nbytes((tm, d), out_dtype)
    return pl.pallas_call(
        _rmsnorm_kernel,
        grid=(t // tm,),
        in_specs=[pl.BlockSpec((tm, d), lambda i: (i, 0)), pl.BlockSpec((1, d), lambda i: (0, 0))],
        out_specs=pl.BlockSpec((tm, d), lambda i: (i, 0)),
        out_shape=jax.ShapeDtypeStruct((t, d), out_dtype),
        compiler_params=_params(("parallel",), nb),
        name="rmsnorm",
    )(x, g.reshape(1, d))


def _rmsnorm_router_kernel(x_ref, g_ref, r_ref, o_ref, idx_ref, wts_ref):
    y = _rmsnorm_rows(x_ref[...], g_ref[...])
    o_ref[...] = y
    n_e = r_ref.shape[1] // 2
    y_hi = y.astype(BF16)
    y_lo = (y - y_hi.astype(F32)).astype(BF16)
    p_hi = jnp.dot(y_hi, r_ref[...], preferred_element_type=F32)
    p_lo = jnp.dot(y_lo, r_ref[...], preferred_element_type=F32)
    logits = p_hi[:, :n_e] + (p_hi[:, n_e:] + p_lo[:, :n_e])
    lane = lax.broadcasted_iota(jnp.int32, logits.shape, 1)
    m1 = jnp.max(logits, axis=-1, keepdims=True)
    i1 = jnp.min(jnp.where(logits == m1, lane, n_e), axis=-1, keepdims=True)
    rest = jnp.where(lane == i1, -jnp.inf, logits)
    m2 = jnp.max(rest, axis=-1, keepdims=True)
    i2 = jnp.min(jnp.where(rest == m2, lane, n_e), axis=-1, keepdims=True)
    e2 = jnp.exp(m2 - m1)
    den = 1.0 + e2
    slot = lax.broadcasted_iota(jnp.int32, idx_ref.shape, 1)
    idx_ref[...] = jnp.where(slot == 0, i1, i2)
    wts_ref[...] = jnp.where(slot == 0, 1.0 / den, e2 / den)


def rmsnorm_router(x, g, router):
    t, d = x.shape
    n_e = router.shape[1]
    tm = _pick(t, 256)
    r_hi = router.astype(BF16)
    r_lo = (router - r_hi.astype(F32)).astype(BF16)
    r_parts = jnp.concatenate([r_hi, r_lo], axis=1)
    nb = 6 * _nbytes((tm, d), F32) + 2 * _nbytes((d, 128), BF16)
    return pl.pallas_call(
        _rmsnorm_router_kernel,
        grid=(t // tm,),
        in_specs=[pl.BlockSpec((tm, d), lambda i: (i, 0)), pl.BlockSpec((1, d), lambda i: (0, 0)),
                  pl.BlockSpec((d, 2 * n_e), lambda i: (0, 0))],
        out_specs=[pl.BlockSpec((tm, d), lambda i: (i, 0)), pl.BlockSpec((tm, TOP_K), lambda i: (i, 0)),
                   pl.BlockSpec((tm, TOP_K), lambda i: (i, 0))],
        out_shape=[jax.ShapeDtypeStruct((t, d), F32), jax.ShapeDtypeStruct((t, TOP_K), jnp.int32),
                   jax.ShapeDtypeStruct((t, TOP_K), F32)],
        compiler_params=_params(("parallel",), nb),
        name="rmsnorm_router",
    )(x, g.reshape(1, d), r_parts)


MXU_COLS = 256
CAST_ROWS = 1024


def _col_chunks(tn):
    cn = MXU_COLS if tn % MXU_COLS == 0 else tn
    return [(c0, cn) for c0 in range(0, tn, cn)]


def _dot_casting(x_ref, rows, stage_ref, wbf_ref, c0, cn):
    k = stage_ref.shape[0]
    kc = CAST_ROWS if k % CAST_ROWS == 0 else k
    acc = None
    for k0 in range(0, k, kc):
        piece = stage_ref[k0:k0 + kc, c0:c0 + cn].astype(BF16)
        wbf_ref[k0:k0 + kc, c0:c0 + cn] = piece
        part = jnp.dot(x_ref[0:rows, k0:k0 + kc], piece, preferred_element_type=F32)
        acc = part if acc is None else acc + part
    return acc


def _matmul_kernel(*refs, staged, has_res, epilogue, tk, tn, nj, k0):
    x_ref, w_ref = refs[0], refs[1]
    pos = 2
    r_ref = None
    if has_res:
        r_ref = refs[pos]
        pos += 1
    o_ref = refs[pos]
    tm = x_ref.shape[0]

    def finish(acc, c0, cn):
        if epilogue == "sigmoid":
            acc = jax.nn.sigmoid(acc)
        if has_res:
            acc = acc + r_ref[:, c0:c0 + cn]
        o_ref[:, c0:c0 + cn] = acc.astype(o_ref.dtype)

    if not staged:
        for c0, cn in _col_chunks(tn):
            finish(jnp.dot(x_ref[...], w_ref[:, c0:c0 + cn], preferred_element_type=F32), c0, cn)
        return

    stage_ref, wbf_ref, sem = refs[pos + 1:pos + 4]
    j = pl.program_id(0)
    i = pl.program_id(1)

    def copy(jj):
        cols = pl.ds(pl.multiple_of(jj * tn, tn), tn)
        return pltpu.make_async_copy(w_ref.at[pl.ds(k0, tk), cols], stage_ref, sem)

    @pl.when(i == 0)
    def _():
        @pl.when(j == 0)
        def _():
            copy(j).start()

        copy(j).wait()
        for c0, cn in _col_chunks(tn):
            finish(_dot_casting(x_ref, tm, stage_ref, wbf_ref, c0, cn), c0, cn)

        @pl.when(j + 1 < nj)
        def _():
            copy(j + 1).start()

    @pl.when(i != 0)
    def _():
        for c0, cn in _col_chunks(tn):
            finish(jnp.dot(x_ref[...], wbf_ref[:, c0:c0 + cn], preferred_element_type=F32), c0, cn)


def matmul(x, w, *, out_dtype, res=None, epilogue=None, tm=1024, tn=1024, k_block=None, w_lead=None):
    m, kx = x.shape
    kw, n = w.shape[-2], w.shape[-1]
    if k_block is None:
        assert kx == kw
        kk, tk = 0, kx
    else:
        kk, tk = k_block
    tm = _pick(m, tm)
    tn = _pick(n, tn)
    nj = n // tn
    staged = w.dtype != BF16
    nb = 2 * _nbytes((tm, tk), BF16) + 2 * _nbytes((tm, tn), out_dtype) + 2 * _nbytes((tm, tn), F32)
    scratch = []
    if staged:
        assert w.ndim == 2
        w_spec = pl.BlockSpec(memory_space=pl.ANY)
        scratch = [pltpu.VMEM((tk, tn), F32), pltpu.VMEM((tk, tn), BF16), pltpu.SemaphoreType.DMA(())]
        nb += _nbytes((tk, tn), F32) + _nbytes((tk, tn), BF16)
    elif w.ndim == 3:
        w_spec = pl.BlockSpec((None, tk, tn), lambda j, i: (w_lead, kk, j))
        nb += 2 * _nbytes((tk, tn), BF16)
    else:
        w_spec = pl.BlockSpec((tk, tn), lambda j, i: (kk, j))
        nb += 2 * _nbytes((tk, tn), BF16)
    in_specs = [pl.BlockSpec((tm, tk), lambda j, i: (i, kk)), w_spec]
    args = [x, w]
    if res is not None:
        in_specs.append(pl.BlockSpec((tm, tn), lambda j, i: (i, j)))
        args.append(res)
        nb += 2 * _nbytes((tm, tn), res.dtype)
    return pl.pallas_call(
        functools.partial(_matmul_kernel, staged=staged, has_res=res is not None, epilogue=epilogue,
                          tk=tk, tn=tn, nj=nj, k0=kk * tk),
        grid=(nj, m // tm),
        in_specs=in_specs,
        out_specs=pl.BlockSpec((tm, tn), lambda j, i: (i, j)),
        out_shape=jax.ShapeDtypeStruct((m, n), out_dtype),
        scratch_shapes=scratch,
        compiler_params=_params(("arbitrary", "arbitrary"), nb),
        name="matmul",
    )(*args)


def swiglu_up(x, w1, w3, *, tm=2048, tn=256):
    tm = _pick(x.shape[0], tm)
    n_tiles = x.shape[0] // tm
    zeros = jnp.zeros((n_tiles,), jnp.int32)
    ones = jnp.ones((n_tiles,), jnp.int32)
    first = zeros.at[0].set(1)
    sched = (zeros, first, zeros, ones, ones, jnp.full((1,), n_tiles, jnp.int32))
    return moe_ffn(x, sched, (w1[None], w3[None]), out_dtype=BF16, tm=tm, tn=tn, name="swiglu_up")


def _merge_kernel(ya_ref, yb_ref, yc_ref, yd_ref, g0_ref, g1_ref, g2_ref, g3_ref, wb_ref, o_ref, wbf_ref):
    @pl.when(pl.program_id(1) == 0)
    def _():
        wbf_ref[...] = wb_ref[...].astype(BF16)

    pairs = ((ya_ref, g0_ref), (yb_ref, g1_ref), (yc_ref, g2_ref), (yd_ref, g3_ref))
    for c0, cn in _col_chunks(o_ref.shape[1]):
        acc = None
        for g, (y_ref, g_ref) in enumerate(pairs):
            proj = jnp.dot(y_ref[...], wbf_ref[g, :, c0:c0 + cn], preferred_element_type=F32)
            term = g_ref[:, c0:c0 + cn].astype(F32) * proj
            acc = term if acc is None else acc + term
        o_ref[:, c0:c0 + cn] = acc.astype(o_ref.dtype)


def merge_branches(branches, gates, w_branch, *, tm=1024, tn=512):
    t, bw = branches[0].shape
    d = w_branch.shape[2]
    tm = _pick(t, tm)
    tn = _pick(d, tn)
    nj = d // tn
    y_spec = pl.BlockSpec((tm, bw), lambda j, i: (i, 0))
    gate_specs = [pl.BlockSpec((tm, tn), functools.partial(lambda j, i, g: (i, g * nj + j), g=g))
                  for g in range(N_BRANCH)]
    nb = (2 * N_BRANCH * _nbytes((tm, bw), BF16) + 2 * N_BRANCH * _nbytes((tm, tn), BF16)
          + 2 * _nbytes((N_BRANCH, bw, tn), F32) + _nbytes((N_BRANCH, bw, tn), BF16)
          + 2 * _nbytes((tm, tn), BF16) + 3 * _nbytes((tm, tn), F32))
    return pl.pallas_call(
        _merge_kernel,
        grid=(nj, t // tm),
        in_specs=[y_spec] * N_BRANCH + gate_specs + [pl.BlockSpec((N_BRANCH, bw, tn), lambda j, i: (0, 0, j))],
        out_specs=pl.BlockSpec((tm, tn), lambda j, i: (i, j)),
        out_shape=jax.ShapeDtypeStruct((t, d), BF16),
        scratch_shapes=[pltpu.VMEM((N_BRANCH, bw, tn), BF16)],
        compiler_params=_params(("arbitrary", "arbitrary"), nb),
        name="merge_branches",
    )(*branches, gates, gates, gates, gates, w_branch)


def _halo_specs(ts, width, col, seq_tiles_total):
    r = ts // HALO
    last = seq_tiles_total * r - 1
    prev = pl.BlockSpec((HALO, width), lambda i: (jnp.maximum(i * r - 1, 0), col))
    cur = pl.BlockSpec((ts, width), lambda i: (i, col))
    nxt = pl.BlockSpec((HALO, width), lambda i: (jnp.minimum((i + 1) * r, last), col))
    return [prev, cur, nxt]


def _edge_flags(ts, seq):
    tiles_per_seq = seq // ts
    k = pl.program_id(0) % tiles_per_seq
    return k == 0, k == tiles_per_seq - 1, k * ts


SUBLANES = 8


def _conv_kernel(lp_ref, lc_ref, ln_ref, gp_ref, gc_ref, gn_ref, cw_ref, cb_ref, lg_ref, lb_ref, o_ref,
                 abuf_ref, ash_ref, ybuf_ref, *, ts, seq):
    first, last, _ = _edge_flags(ts, seq)
    width = lc_ref.shape[1]
    glu = lambda l_ref, g_ref: l_ref[...].astype(F32) * jax.nn.sigmoid(g_ref[...].astype(F32))
    abuf_ref[0:HALO, :] = jnp.where(first, 0.0, glu(lp_ref, gp_ref))
    abuf_ref[HALO + ts:2 * HALO + ts, :] = jnp.where(last, 0.0, glu(ln_ref, gn_ref))
    abuf_ref[HALO:HALO + ts, :] = glu(lc_ref, gc_ref)

    span = ash_ref.shape[1]
    for p in range(1, SUBLANES):
        ash_ref[p - 1] = abuf_ref[p:p + span, :]

    rows = 64 if ts % 64 == 0 else ts
    for r0 in range(0, ts, rows):
        for c0 in range(0, width, 128):
            acc = jnp.zeros((rows, 128), F32)
            for k in range(CONV_WIDTH):
                q, p = divmod(HALO - CONV_PAD + k, SUBLANES)
                start = r0 + q * SUBLANES
                src = abuf_ref if p == 0 else ash_ref.at[p - 1]
                acc = acc + cw_ref[k:k + 1, c0:c0 + 128] * src[start:start + rows, c0:c0 + 128]
            ybuf_ref[r0:r0 + rows, c0:c0 + 128] = acc + cb_ref[:, c0:c0 + 128]

    y = ybuf_ref[...]
    mu = jnp.mean(y, axis=-1, keepdims=True)
    yc = y - mu
    var = jnp.mean(yc * yc, axis=-1, keepdims=True)
    z = yc * lax.rsqrt(var + EPS) * lg_ref[...] + lb_ref[...]
    o_ref[...] = jax.nn.silu(z).astype(o_ref.dtype)


def conv_branch(hin, conv_w, conv_b, ln_g, ln_b, *, seq, bw, ts=256):
    t = hin.shape[0]
    ts = _pick(seq, ts)
    n_tiles = t // ts
    row = lambda v: v.reshape(1, bw)
    const = lambda shape: pl.BlockSpec(shape, lambda i: (0, 0))
    span = ts + 2 * HALO - SUBLANES
    nb = (4 * _nbytes((ts + 2 * HALO, bw), BF16) + 2 * _nbytes((ts, bw), BF16) + _nbytes((ts + 2 * HALO, bw), F32)
          + (SUBLANES - 1) * _nbytes((span, bw), F32) + 5 * _nbytes((ts, bw), F32))
    return pl.pallas_call(
        functools.partial(_conv_kernel, ts=ts, seq=seq),
        grid=(n_tiles,),
        in_specs=_halo_specs(ts, bw, 0, n_tiles) + _halo_specs(ts, bw, 1, n_tiles)
        + [const((CONV_WIDTH, bw)), const((1, bw)), const((1, bw)), const((1, bw))],
        out_specs=pl.BlockSpec((ts, bw), lambda i: (i, 0)),
        out_shape=jax.ShapeDtypeStruct((t, bw), BF16),
        scratch_shapes=[pltpu.VMEM((ts + 2 * HALO, bw), F32), pltpu.VMEM((SUBLANES - 1, span, bw), F32),
                        pltpu.VMEM((ts, bw), F32)],
        compiler_params=_params(("parallel",), nb),
        name="conv_branch",
    )(hin, hin, hin, hin, hin, hin, conv_w, row(conv_b), row(ln_g), row(ln_b))


def _pool_kernel(pp_ref, pc_ref, pn_ref, pw_ref, ps_ref, o_ref, pbuf_ref, *, ts, seq):
    first, last, pos0 = _edge_flags(ts, seq)
    n_g = len(POOL_WINDOWS)
    pg = pc_ref.shape[1] // n_g
    pbuf_ref[0:HALO, :] = jnp.where(first, 0.0, pp_ref[...].astype(F32))
    pbuf_ref[HALO + ts:2 * HALO + ts, :] = jnp.where(last, 0.0, pn_ref[...].astype(F32))
    pbuf_ref[HALO:HALO + ts, :] = pc_ref[...].astype(F32)
    pos = pos0 + lax.broadcasted_iota(jnp.int32, (ts, 1), 0)
    for g, w in enumerate(POOL_WINDOWS):
        cols = slice(g * pg, (g + 1) * pg)
        win = None
        for d in range(-(w // 2), w // 2):
            term = pbuf_ref[HALO + d:HALO + d + ts, cols]
            win = term if win is None else win + term
        lo = jnp.maximum(pos - w // 2, 0)
        hi = jnp.minimum(pos + w // 2 - 1, seq - 1)
        cnt = (hi - lo + 1).astype(F32)
        pooled = win / cnt - pbuf_ref[HALO:HALO + ts, cols]
        mixed = jnp.dot(pooled.astype(BF16), pw_ref[g].astype(BF16), preferred_element_type=F32)
        o_ref[:, cols] = (mixed * ps_ref[:, cols]).astype(o_ref.dtype)


def pool_branch(hin, pool_w, pool_scale, *, seq, bw, col, ts=256):
    t = hin.shape[0]
    ts = _pick(seq, ts)
    n_tiles = t // ts
    nb = 2 * _nbytes((ts + 2 * HALO, bw), F32) + 2 * _nbytes((ts, bw), BF16) + 4 * _nbytes((ts + 2 * HALO, bw), F32)
    nb += 2 * _nbytes(pool_w.shape, F32)
    return pl.pallas_call(
        functools.partial(_pool_kernel, ts=ts, seq=seq),
        grid=(n_tiles,),
        in_specs=_halo_specs(ts, bw, col, n_tiles)
        + [pl.BlockSpec(pool_w.shape, lambda i: (0, 0, 0)), pl.BlockSpec((1, bw), lambda i: (0, 0))],
        out_specs=pl.BlockSpec((ts, bw), lambda i: (i, 0)),
        out_shape=jax.ShapeDtypeStruct((t, bw), BF16),
        scratch_shapes=[pltpu.VMEM((ts + 2 * HALO, bw), F32)],
        compiler_params=_params(("parallel",), nb),
        name="pool_branch",
    )(hin, hin, hin, pool_w, pool_scale.reshape(1, bw))


def _dft_tables(seq, group):
    c = np.arange(group)
    ang = 2.0 * np.pi * np.outer(c, c) / group
    scale = 1.0 / np.sqrt(float(seq) * group)
    chan = np.concatenate([np.cos(ang), np.sin(ang)], axis=1) * scale
    fa = 64 if seq % 64 == 0 else 1
    sp = np.arange(seq)[:, None]
    ang_a = 2.0 * np.pi * ((sp * fa * np.arange(seq // fa)[None, :]) % seq) / seq
    ang_b = 2.0 * np.pi * ((sp * np.arange(fa)[None, :]) % seq) / seq
    ca, sa = jnp.asarray(np.cos(ang_a), F32)[:, :, None], jnp.asarray(np.sin(ang_a), F32)[:, :, None]
    cb, sb = jnp.asarray(np.cos(ang_b), F32)[:, None, :], jnp.asarray(np.sin(ang_b), F32)[:, None, :]
    cos_s = (ca * cb - sa * sb).reshape(seq, seq)
    sin_s = (sa * cb + ca * sb).reshape(seq, seq)
    pos = jnp.concatenate([cos_s, -sin_s], axis=1).astype(BF16)
    return jnp.asarray(chan, F32).astype(BF16), pos


def _chan_dft_kernel(f_ref, tbl_ref, o_ref):
    n_g = N_FFT_GROUPS
    fg = f_ref.shape[1] // n_g
    tbl = tbl_ref[...]
    for g in range(n_g):
        r = jnp.dot(f_ref[:, g * fg:(g + 1) * fg].astype(BF16), tbl, preferred_element_type=F32)
        o_ref[0, :, g * fg:(g + 1) * fg] = r[:, :fg].astype(o_ref.dtype)
        o_ref[1, :, g * fg:(g + 1) * fg] = r[:, fg:].astype(o_ref.dtype)


def fourier_branch(hin, chan_tbl, pos_tbl, *, batch, seq, bw, col, ts=512):
    t = hin.shape[0]
    ts = _pick(seq, ts)
    tiles_per_seq = seq // ts
    nb = 2 * _nbytes((ts, bw), F32) + 4 * _nbytes((ts, bw), BF16) + 2 * _nbytes(chan_tbl.shape, BF16)
    nb += 4 * _nbytes((ts, bw), F32)
    proj = pl.pallas_call(
        _chan_dft_kernel,
        grid=(t // ts,),
        in_specs=[pl.BlockSpec((ts, bw), lambda i: (i, col)), pl.BlockSpec(chan_tbl.shape, lambda i: (0, 0))],
        out_specs=pl.BlockSpec((None, 2, ts, bw), lambda i: (i // tiles_per_seq, 0, i % tiles_per_seq, 0)),
        out_shape=jax.ShapeDtypeStruct((batch, 2, seq, bw), BF16),
        compiler_params=_params(("parallel",), nb),
        name="fourier_channels",
    )(hin, chan_tbl)
    proj = proj.reshape(batch, 2 * seq, bw)
    outs = [matmul(pos_tbl, proj, out_dtype=BF16, tm=512, tn=512, w_lead=b) for b in range(batch)]
    return jnp.concatenate(outs, axis=0)


def _gmlp_kernel(u_ref, v_ref, lg_ref, lb_ref, ws_ref, bias_ref, o_ref, *, ts):
    v = v_ref[...].astype(F32)
    mu = jnp.mean(v, axis=-1, keepdims=True)
    vc = v - mu
    var = jnp.mean(vc * vc, axis=-1, keepdims=True)
    vn = (vc * lax.rsqrt(var + EPS) * lg_ref[...] + lb_ref[...]).astype(BF16)
    n_chunks = ts // GMLP_CHUNK
    n_heads = v.shape[1] // GMLP_HEAD_DIM
    for h in range(n_heads):
        cols = slice(h * GMLP_HEAD_DIM, (h + 1) * GMLP_HEAD_DIM)
        rhs = jnp.concatenate([vn[c * GMLP_CHUNK:(c + 1) * GMLP_CHUNK, cols] for c in range(n_chunks)], axis=1)
        s = jnp.dot(ws_ref[h].astype(BF16), rhs, preferred_element_type=F32)
        for c in range(n_chunks):
            rows = slice(c * GMLP_CHUNK, (c + 1) * GMLP_CHUNK)
            sc = s[:, c * GMLP_HEAD_DIM:(c + 1) * GMLP_HEAD_DIM] + bias_ref[:, cols]
            o_ref[rows, cols] = (u_ref[rows, cols].astype(F32) * sc).astype(o_ref.dtype)


def gmlp_branch(hin, ln_g, ln_b, ws, bias, *, bw, col_u, col_v, ts=512):
    t = hin.shape[0]
    ts = _pick(t, ts)
    assert ts % GMLP_CHUNK == 0
    n_heads = bw // GMLP_HEAD_DIM
    bias_rows = jnp.repeat(bias.T, GMLP_HEAD_DIM, axis=1)
    nb = 4 * _nbytes((ts, bw), F32) + 2 * _nbytes((ts, bw), BF16) + 6 * _nbytes((ts, bw), F32)
    return pl.pallas_call(
        functools.partial(_gmlp_kernel, ts=ts),
        grid=(t // ts,),
        in_specs=[pl.BlockSpec((ts, bw), lambda i: (i, col_u)), pl.BlockSpec((ts, bw), lambda i: (i, col_v)),
                  pl.BlockSpec((1, bw), lambda i: (0, 0)), pl.BlockSpec((1, bw), lambda i: (0, 0)),
                  pl.BlockSpec((n_heads, GMLP_CHUNK, GMLP_CHUNK), lambda i: (0, 0, 0)),
                  pl.BlockSpec((GMLP_CHUNK, bw), lambda i: (0, 0))],
        out_specs=pl.BlockSpec((ts, bw), lambda i: (i, 0)),
        out_shape=jax.ShapeDtypeStruct((t, bw), BF16),
        compiler_params=_params(("parallel",), nb),
        name="gmlp_branch",
    )(hin, hin, ln_g.reshape(1, bw), ln_b.reshape(1, bw), ws, bias_rows)


def _route(idx, tm):
    a = idx.size
    n_tiles = a // tm + N_EXPERTS
    experts = jnp.arange(N_EXPERTS, dtype=jnp.int32)
    e_flat = idx.reshape(a)
    onehot = (e_flat[:, None] == experts[None, :]).astype(jnp.int32)
    csum = jnp.cumsum(onehot, axis=0)
    counts = csum[-1]
    rank = jnp.sum((csum - onehot) * onehot, axis=1)
    tiles_e = (counts + tm - 1) // tm
    tile_end = jnp.cumsum(tiles_e)
    tile_start = tile_end - tiles_e
    pos = jnp.sum(onehot * tile_start[None, :], axis=1) * tm + rank
    src_tok = jnp.zeros((n_tiles * tm,), jnp.int32).at[pos].set(jnp.arange(a, dtype=jnp.int32) // TOP_K)
    n_used = tile_end[-1]
    tile_ids = jnp.minimum(jnp.arange(n_tiles, dtype=jnp.int32), n_used - 1)
    tile_expert = jnp.sum((tile_ids[:, None] >= tile_end[None, :]).astype(jnp.int32), axis=1)
    first = jnp.concatenate([jnp.ones((1,), jnp.int32), (tile_expert[1:] != tile_expert[:-1]).astype(jnp.int32)])
    present = tiles_e > 0
    first_present = jnp.min(jnp.where(present, experts, N_EXPERTS))
    last_present = jnp.max(jnp.where(present, experts, -1))
    later = jnp.logical_and(present[None, :], experts[None, :] > experts[:, None])
    next_present = jnp.min(jnp.where(later, experts[None, :], N_EXPERTS), axis=1)
    next_present = jnp.where(next_present == N_EXPERTS, first_present, next_present)
    nxt = next_present[tile_expert]
    last_group = (tile_expert == last_present).astype(jnp.int32)
    rows_valid = counts[tile_expert] - (tile_ids - tile_start[tile_expert]) * tm
    full = (rows_valid > tm // 2).astype(jnp.int32)
    sched = (tile_expert, first, nxt, last_group, full, n_used.reshape(1).astype(jnp.int32))
    return pos, src_tok, sched


def _row_copy(src_hbm, row, dst_ref, r, sem):
    return pltpu.make_async_copy(src_hbm.at[pl.ds(row, 1), :], dst_ref.at[pl.ds(r, 1), :], sem)


ROW_DMA_UNROLL = 8


def _gather_rows_kernel(tok_ref, nrows_ref, src_hbm, o_ref, buf_ref, sem, *, tg, n_steps):
    i = pl.program_id(0)

    def start_step(step):
        slot = step % 2

        def issue(r, c):
            _row_copy(src_hbm, tok_ref[step * tg + r], buf_ref.at[slot], r, sem.at[slot]).start()
            return c

        lax.fori_loop(0, tg, issue, 0, unroll=ROW_DMA_UNROLL)

    def active(step):
        return step * tg < nrows_ref[0]

    @pl.when(i == 0)
    def _():
        start_step(i)

    @pl.when(jnp.logical_and(i + 1 < n_steps, active(i + 1)))
    def _():
        start_step(i + 1)

    @pl.when(active(i))
    def _():
        slot = i % 2

        def drain(r, c):
            _row_copy(src_hbm, 0, buf_ref.at[slot], r, sem.at[slot]).wait()
            return c

        lax.fori_loop(0, tg, drain, 0, unroll=ROW_DMA_UNROLL)
        o_ref[...] = buf_ref[slot].astype(o_ref.dtype)

    @pl.when(jnp.logical_not(active(i)))
    def _():
        o_ref[...] = jnp.zeros_like(o_ref)


def gather_rows(src, tok, n_rows, *, out_dtype, tg=256):
    p = tok.shape[0]
    d = src.shape[1]
    tg = _pick(p, tg)
    n_steps = p // tg
    nb = 2 * _nbytes((tg, d), F32) + 2 * _nbytes((tg, d), out_dtype) + _nbytes((tg, d), F32)
    return pl.pallas_call(
        functools.partial(_gather_rows_kernel, tg=tg, n_steps=n_steps),
        grid_spec=pltpu.PrefetchScalarGridSpec(
            num_scalar_prefetch=2,
            grid=(n_steps,),
            in_specs=[pl.BlockSpec(memory_space=pl.ANY)],
            out_specs=pl.BlockSpec((tg, d), lambda i, tok, nrows: (i, 0)),
            scratch_shapes=[pltpu.VMEM((2, tg, d), F32), pltpu.SemaphoreType.DMA((2,))],
        ),
        out_shape=jax.ShapeDtypeStruct((p, d), out_dtype),
        compiler_params=_params(("arbitrary",), nb),
        name="gather_rows",
    )(tok, n_rows, src)


def _moe_ffn_kernel(te_ref, first_ref, nxt_ref, lastg_ref, full_ref, nu_ref, x_ref, *refs, n_w, tn, nj, tm):
    w_hbm = refs[:n_w]
    o_ref = refs[n_w]
    stage = refs[n_w + 1:2 * n_w + 1]
    wbf = refs[2 * n_w + 1:3 * n_w + 1]
    sem = refs[3 * n_w + 1]
    j = pl.program_id(0)
    n = pl.program_id(1)
    used = n < nu_ref[0]

    def copies(e, jj):
        cols = pl.ds(pl.multiple_of(jj * tn, tn), tn)
        return [pltpu.make_async_copy(w_hbm[q].at[e, :, cols], stage[q], sem.at[q]) for q in range(n_w)]

    def compute(rows, dot_w):
        for c0, cn in _col_chunks(tn):
            acc = dot_w(0, rows, c0, cn)
            if n_w == 2:
                acc = jax.nn.silu(acc) * dot_w(1, rows, c0, cn)
            o_ref[0:rows, c0:c0 + cn] = acc.astype(o_ref.dtype)

    def dot_ready(q, rows, c0, cn):
        return jnp.dot(x_ref[0:rows, :], wbf[q][:, c0:c0 + cn], preferred_element_type=F32)

    def dot_casting(q, rows, c0, cn):
        return _dot_casting(x_ref, rows, stage[q], wbf[q], c0, cn)

    is_first = first_ref[n] == 1
    is_full = full_ref[n] == 1

    @pl.when(jnp.logical_and(used, is_first))
    def _():
        @pl.when(jnp.logical_and(j == 0, n == 0))
        def _():
            for c in copies(te_ref[n], j):
                c.start()

        for c in copies(te_ref[n], j):
            c.wait()
        compute(tm, dot_casting)
        is_last = lastg_ref[n] == 1

        @pl.when(jnp.logical_not(is_last))
        def _():
            for c in copies(nxt_ref[n], j):
                c.start()

        @pl.when(jnp.logical_and(is_last, j + 1 < nj))
        def _():
            for c in copies(nxt_ref[n], j + 1):
                c.start()

    later = jnp.logical_and(used, jnp.logical_not(is_first))

    @pl.when(jnp.logical_and(later, is_full))
    def _():
        compute(tm, dot_ready)

    @pl.when(jnp.logical_and(later, jnp.logical_not(is_full)))
    def _():
        compute(tm // 2, dot_ready)
        o_ref[tm // 2:tm, :] = jnp.zeros((tm - tm // 2, tn), o_ref.dtype)

    @pl.when(jnp.logical_not(used))
    def _():
        o_ref[...] = jnp.zeros_like(o_ref)


def moe_ffn(xs, sched, weights, *, out_dtype, tm, tn=512, name="moe_ffn"):
    p, k = xs.shape
    n_w = len(weights)
    n_out = weights[0].shape[2]
    tn = _pick(n_out, tn)
    nj = n_out // tn
    x_spec = pl.BlockSpec((tm, k), lambda j, n, te, fi, nx, lg, fu, nu: (jnp.minimum(n, nu[0] - 1), 0))
    o_spec = pl.BlockSpec((tm, tn), lambda j, n, te, fi, nx, lg, fu, nu: (n, j))
    nb = (2 * _nbytes((tm, k), BF16) + n_w * _nbytes((k, tn), F32) + n_w * _nbytes((k, tn), BF16)
          + 2 * _nbytes((tm, tn), out_dtype) + (n_w + 1) * _nbytes((tm, tn), F32))
    scratch = ([pltpu.VMEM((k, tn), F32)] * n_w + [pltpu.VMEM((k, tn), BF16)] * n_w
               + [pltpu.SemaphoreType.DMA((n_w,))])
    return pl.pallas_call(
        functools.partial(_moe_ffn_kernel, n_w=n_w, tn=tn, nj=nj, tm=tm),
        grid_spec=pltpu.PrefetchScalarGridSpec(
            num_scalar_prefetch=6,
            grid=(nj, p // tm),
            in_specs=[x_spec] + [pl.BlockSpec(memory_space=pl.ANY)] * n_w,
            out_specs=o_spec,
            scratch_shapes=scratch,
        ),
        out_shape=jax.ShapeDtypeStruct((p, n_out), out_dtype),
        compiler_params=_params(("arbitrary", "arbitrary"), nb),
        name=name,
    )(*sched, xs, *weights)


def _combine_norm_kernel(pos_ref, h_ref, wts_ref, ys_hbm, g_ref, o_ref, buf_ref, sem, *, tg, n_steps):
    i = pl.program_id(0)

    def start_step(step):
        slot = step % 2

        def issue(r, c):
            for k in range(TOP_K):
                row = pos_ref[(step * tg + r) * TOP_K + k]
                _row_copy(ys_hbm, row, buf_ref.at[slot, k], r, sem.at[slot]).start()
            return c

        lax.fori_loop(0, tg, issue, 0, unroll=ROW_DMA_UNROLL)

    @pl.when(i == 0)
    def _():
        start_step(i)

    @pl.when(i + 1 < n_steps)
    def _():
        start_step(i + 1)

    slot = i % 2

    def drain(r, c):
        for k in range(TOP_K):
            _row_copy(ys_hbm, 0, buf_ref.at[slot, k], r, sem.at[slot]).wait()
        return c

    lax.fori_loop(0, tg, drain, 0, unroll=ROW_DMA_UNROLL)
    h = h_ref[...]
    for k in range(TOP_K):
        h = h + wts_ref[:, k:k + 1] * buf_ref[slot, k]
    o_ref[...] = _rmsnorm_rows(h, g_ref[...]).astype(o_ref.dtype)


def combine_norm(h, wts, pos, ys, g, *, tg=128):
    t, d = h.shape
    tg = _pick(t, tg)
    n_steps = t // tg
    nb = 4 * _nbytes((tg, d), F32) + 2 * TOP_K * _nbytes((tg, d), F32) + 4 * _nbytes((tg, d), F32)
    return pl.pallas_call(
        functools.partial(_combine_norm_kernel, tg=tg, n_steps=n_steps),
        grid_spec=pltpu.PrefetchScalarGridSpec(
            num_scalar_prefetch=1,
            grid=(n_steps,),
            in_specs=[pl.BlockSpec((tg, d), lambda i, pos: (i, 0)), pl.BlockSpec((tg, TOP_K), lambda i, pos: (i, 0)),
                      pl.BlockSpec(memory_space=pl.ANY), pl.BlockSpec((1, d), lambda i, pos: (0, 0))],
            out_specs=pl.BlockSpec((tg, d), lambda i, pos: (i, 0)),
            scratch_shapes=[pltpu.VMEM((2, TOP_K, tg, d), F32), pltpu.SemaphoreType.DMA((2,))],
        ),
        out_shape=jax.ShapeDtypeStruct((t, d), F32),
        compiler_params=_params(("arbitrary",), nb),
        name="combine_norm",
    )(pos, h, wts, ys, g.reshape(1, d))


def moe_ffn_then_norm(h, norm_g, router, w1, w3, w2, final_g, *, tm=512):
    t = h.shape[0]
    tm = _pick(t * TOP_K, tm)
    hn, idx, wts = rmsnorm_router(h, norm_g, router)
    pos, src_tok, sched = _route(idx, tm)
    xs = gather_rows(hn, src_tok, sched[-1] * tm, out_dtype=BF16)
    acts = moe_ffn(xs, sched, (w1, w3), out_dtype=BF16, tm=tm, name="moe_up")
    ys = moe_ffn(acts, sched, (w2,), out_dtype=F32, tm=tm, tn=1024, name="moe_down")
    return combine_norm(h, wts, pos, ys, final_g)


def _mixer(h, norm1, w_in, w_gate, conv_w, conv_b, conv_ln_g, conv_ln_b, pool_w, pool_scale, gmlp_ln_g, gmlp_ln_b,
           gmlp_ws, gmlp_b, w_branch, w_out, tables, *, batch, seq):
    bw = w_branch.shape[1]
    xn = rmsnorm(h, norm1, BF16)
    hin = matmul(xn, w_in, out_dtype=BF16)
    ya = conv_branch(hin, conv_w, conv_b, conv_ln_g, conv_ln_b, seq=seq, bw=bw)
    yb = pool_branch(hin, pool_w, pool_scale, seq=seq, bw=bw, col=2)
    yc = fourier_branch(hin, *tables, batch=batch, seq=seq, bw=bw, col=3)
    yd = gmlp_branch(hin, gmlp_ln_g, gmlp_ln_b, gmlp_ws, gmlp_b, bw=bw, col_u=4, col_v=5)
    gates = matmul(xn, w_gate, out_dtype=BF16, epilogue="sigmoid")
    merged = merge_branches((ya, yb, yc, yd), gates, w_branch)
    return matmul(merged, w_out, out_dtype=F32, res=h, tm=512)


def kernel(x, l0_norm1, l0_w_in, l0_w_gate, l0_conv_w, l0_conv_b, l0_conv_ln_g, l0_conv_ln_b, l0_pool_w, l0_pool_scale, l0_gmlp_ln_g, l0_gmlp_ln_b, l0_gmlp_ws, l0_gmlp_b, l0_w_branch, l0_w_out, l0_norm2, l0_ffn_w1, l0_ffn_w3, l0_ffn_w2, l1_norm1, l1_w_in, l1_w_gate, l1_conv_w, l1_conv_b, l1_conv_ln_g, l1_conv_ln_b, l1_pool_w, l1_pool_scale, l1_gmlp_ln_g, l1_gmlp_ln_b, l1_gmlp_ws, l1_gmlp_b, l1_w_branch, l1_w_out, l1_norm2, l1_router, l1_exp_w1, l1_exp_w3, l1_exp_w2, final_norm):
    batch, seq, d = x.shape
    bw = l0_w_branch.shape[1]
    tables = _dft_tables(seq, bw // N_FFT_GROUPS)
    h = x.reshape(batch * seq, d)

    h = _mixer(h, l0_norm1, l0_w_in, l0_w_gate, l0_conv_w, l0_conv_b, l0_conv_ln_g, l0_conv_ln_b, l0_pool_w,
               l0_pool_scale, l0_gmlp_ln_g, l0_gmlp_ln_b, l0_gmlp_ws, l0_gmlp_b, l0_w_branch, l0_w_out, tables,
               batch=batch, seq=seq)
    hn = rmsnorm(h, l0_norm2, BF16)
    acts = swiglu_up(hn, l0_ffn_w1, l0_ffn_w3)
    half = l0_ffn_w2.shape[0] // 2
    h = matmul(acts, l0_ffn_w2, out_dtype=F32, res=h, tn=512, k_block=(0, half))
    h = matmul(acts, l0_ffn_w2, out_dtype=F32, res=h, tn=512, k_block=(1, half))

    h = _mixer(h, l1_norm1, l1_w_in, l1_w_gate, l1_conv_w, l1_conv_b, l1_conv_ln_g, l1_conv_ln_b, l1_pool_w,
               l1_pool_scale, l1_gmlp_ln_g, l1_gmlp_ln_b, l1_gmlp_ws, l1_gmlp_b, l1_w_branch, l1_w_out, tables,
               batch=batch, seq=seq)
    out = moe_ffn_then_norm(h, l1_norm2, l1_router, l1_exp_w1, l1_exp_w3, l1_exp_w2, final_norm)
    return out.reshape(batch, seq, d)
```

```python
import functools

import numpy as np
import jax
import jax.numpy as jnp
from jax import lax
from jax.experimental import pallas as pl
from jax.experimental.pallas import tpu as pltpu

F32 = jnp.float32
BF16 = jnp.bfloat16

N_BRANCH = 4
CONV_WIDTH = 31
CONV_PAD = CONV_WIDTH // 2
POOL_WINDOWS = (2, 4, 8, 16)
N_FFT_GROUPS = 4
GMLP_CHUNK = 128
GMLP_HEAD_DIM = 128
N_EXPERTS = 8
TOP_K = 2
EPS = 1e-6

HALO = 16
V7X_VMEM_BYTES = 64 * 1024 * 1024
VMEM_CAP = V7X_VMEM_BYTES - 6 * 1024 * 1024


def _vmem_limit(nbytes):
    return int(min(VMEM_CAP, nbytes + nbytes // 4 + (4 << 20)))


def _nbytes(shape, dtype):
    return int(np.prod(shape)) * jnp.dtype(dtype).itemsize


def _params(sem, nbytes):
    return pltpu.CompilerParams(dimension_semantics=sem, vmem_limit_bytes=_vmem_limit(nbytes))


def _pick(n, pref):
    t = min(n, pref)
    while n % t:
        t -= 8
    return t


def _rmsnorm_rows(x, g):
    return x * lax.rsqrt(jnp.mean(x * x, axis=-1, keepdims=True) + EPS) * g


def _rmsnorm_kernel(x_ref, g_ref, o_ref):
    o_ref[...] = _rmsnorm_rows(x_ref[...], g_ref[...]).astype(o_ref.dtype)


def rmsnorm(x, g, out_dtype):
    t, d = x.shape
    tm = _pick(t, 256)
    nb = 2 * _nbytes((tm, d), F32) + 2 * _nbytes((tm, d), out_dtype)
    return pl.pallas_call(
        _rmsnorm_kernel,
        grid=(t // tm,),
        in_specs=[pl.BlockSpec((tm, d), lambda i: (i, 0)), pl.BlockSpec((1, d), lambda i: (0, 0))],
        out_specs=pl.BlockSpec((tm, d), lambda i: (i, 0)),
        out_shape=jax.ShapeDtypeStruct((t, d), out_dtype),
        compiler_params=_params(("parallel",), nb),
        name="rmsnorm",
    )(x, g.reshape(1, d))


def _rmsnorm_router_kernel(x_ref, g_ref, r_ref, o_ref, idx_ref, wts_ref):
    y = _rmsnorm_rows(x_ref[...], g_ref[...])
    o_ref[...] = y
    n_e = r_ref.shape[1] // 2
    y_hi = y.astype(BF16)
    y_lo = (y - y_hi.astype(F32)).astype(BF16)
    p_hi = jnp.dot(y_hi, r_ref[...], preferred_element_type=F32)
    p_lo = jnp.dot(y_lo, r_ref[...], preferred_element_type=F32)
    logits = p_hi[:, :n_e] + (p_hi[:, n_e:] + p_lo[:, :n_e])
    lane = lax.broadcasted_iota(jnp.int32, logits.shape, 1)
    m1 = jnp.max(logits, axis=-1, keepdims=True)
    i1 = jnp.min(jnp.where(logits == m1, lane, n_e), axis=-1, keepdims=True)
    rest = jnp.where(lane == i1, -jnp.inf, logits)
    m2 = jnp.max(rest, axis=-1, keepdims=True)
    i2 = jnp.min(jnp.where(rest == m2, lane, n_e), axis=-1, keepdims=True)
    e2 = jnp.exp(m2 - m1)
    den = 1.0 + e2
    slot = lax.broadcasted_iota(jnp.int32, idx_ref.shape, 1)
    idx_ref[...] = jnp.where(slot == 0, i1, i2)
    wts_ref[...] = jnp.where(slot == 0, 1.0 / den, e2 / den)


def rmsnorm_router(x, g, router):
    t, d = x.shape
    n_e = router.shape[1]
    tm = _pick(t, 256)
    r_hi = router.astype(BF16)
    r_lo = (router - r_hi.astype(F32)).astype(BF16)
    r_parts = jnp.concatenate([r_hi, r_lo], axis=1)
    nb = 6 * _nbytes((tm, d), F32) + 2 * _nbytes((d, 128), BF16)
    return pl.pallas_call(
        _rmsnorm_router_kernel,
        grid=(t // tm,),
        in_specs=[pl.BlockSpec((tm, d), lambda i: (i, 0)), pl.BlockSpec((1, d), lambda i: (0, 0)),
                  pl.BlockSpec((d, 2 * n_e), lambda i: (0, 0))],
        out_specs=[pl.BlockSpec((tm, d), lambda i: (i, 0)), pl.BlockSpec((tm, TOP_K), lambda i: (i, 0)),
                   pl.BlockSpec((tm, TOP_K), lambda i: (i, 0))],
        out_shape=[jax.ShapeDtypeStruct((t, d), F32), jax.ShapeDtypeStruct((t, TOP_K), jnp.int32),
                   jax.ShapeDtypeStruct((t, TOP_K), F32)],
        compiler_params=_params(("parallel",), nb),
        name="rmsnorm_router",
    )(x, g.reshape(1, d), r_parts)


MXU_COLS = 256
CAST_ROWS = 128


def _col_chunks(tn):
    cn = MXU_COLS if tn % MXU_COLS == 0 else tn
    return [(c0, cn) for c0 in range(0, tn, cn)]


def _cast_staged(stage_refs, wbf_refs):
    k = stage_refs[0].shape[0]

    def body(c, carry):
        rows = pl.ds(pl.multiple_of(c * CAST_ROWS, CAST_ROWS), CAST_ROWS)
        for stage_ref, wbf_ref in zip(stage_refs, wbf_refs):
            wbf_ref[rows, :] = stage_ref[rows, :].astype(BF16)
        return carry

    lax.fori_loop(0, k // CAST_ROWS, body, 0)


def _matmul_kernel(*refs, staged, has_res, epilogue, tk, tn, nj, k0):
    x_ref, w_ref = refs[0], refs[1]
    pos = 2
    r_ref = None
    if has_res:
        r_ref = refs[pos]
        pos += 1
    o_ref = refs[pos]
    if staged:
        stage_ref, wbf_ref, sem = refs[pos + 1:pos + 4]
        j = pl.program_id(0)

        def copy(jj):
            cols = pl.ds(pl.multiple_of(jj * tn, tn), tn)
            return pltpu.make_async_copy(w_ref.at[pl.ds(k0, tk), cols], stage_ref, sem)

        @pl.when(pl.program_id(1) == 0)
        def _():
            @pl.when(j == 0)
            def _():
                copy(j).start()

            copy(j).wait()
            _cast_staged([stage_ref], [wbf_ref])

            @pl.when(j + 1 < nj)
            def _():
                copy(j + 1).start()

        w = wbf_ref[...]
    else:
        w = w_ref[...]
    acc = jnp.dot(x_ref[...], w, preferred_element_type=F32)
    if epilogue == "sigmoid":
        acc = jax.nn.sigmoid(acc)
    if has_res:
        acc = acc + r_ref[...]
    o_ref[...] = acc.astype(o_ref.dtype)


def matmul(x, w, *, out_dtype, res=None, epilogue=None, tm=1024, tn=1024, k_block=None, w_lead=None):
    m, kx = x.shape
    kw, n = w.shape[-2], w.shape[-1]
    if k_block is None:
        assert kx == kw
        kk, tk = 0, kx
    else:
        kk, tk = k_block
    tm = _pick(m, tm)
    tn = _pick(n, tn)
    nj = n // tn
    staged = w.dtype != BF16
    nb = 2 * _nbytes((tm, tk), BF16) + 2 * _nbytes((tm, tn), out_dtype) + 2 * _nbytes((tm, tn), F32)
    scratch = []
    if staged:
        assert w.ndim == 2
        w_spec = pl.BlockSpec(memory_space=pl.ANY)
        scratch = [pltpu.VMEM((tk, tn), F32), pltpu.VMEM((tk, tn), BF16), pltpu.SemaphoreType.DMA(())]
        nb += _nbytes((tk, tn), F32) + _nbytes((tk, tn), BF16)
    elif w.ndim == 3:
        w_spec = pl.BlockSpec((None, tk, tn), lambda j, i: (w_lead, kk, j))
        nb += 2 * _nbytes((tk, tn), BF16)
    else:
        w_spec = pl.BlockSpec((tk, tn), lambda j, i: (kk, j))
        nb += 2 * _nbytes((tk, tn), BF16)
    in_specs = [pl.BlockSpec((tm, tk), lambda j, i: (i, kk)), w_spec]
    args = [x, w]
    if res is not None:
        in_specs.append(pl.BlockSpec((tm, tn), lambda j, i: (i, j)))
        args.append(res)
        nb += 2 * _nbytes((tm, tn), res.dtype)
    return pl.pallas_call(
        functools.partial(_matmul_kernel, staged=staged, has_res=res is not None, epilogue=epilogue,
                          tk=tk, tn=tn, nj=nj, k0=kk * tk),
        grid=(nj, m // tm),
        in_specs=in_specs,
        out_specs=pl.BlockSpec((tm, tn), lambda j, i: (i, j)),
        out_shape=jax.ShapeDtypeStruct((m, n), out_dtype),
        scratch_shapes=scratch,
        compiler_params=_params(("arbitrary", "arbitrary"), nb),
        name="matmul",
    )(*args)


def swiglu_up(x, w1, w3, *, tm=2048, tn=256):
    tm = _pick(x.shape[0], tm)
    n_tiles = x.shape[0] // tm
    zeros = jnp.zeros((n_tiles,), jnp.int32)
    ones = jnp.ones((n_tiles,), jnp.int32)
    first = zeros.at[0].set(1)
    sched = (zeros, first, zeros, ones, ones, jnp.full((1,), n_tiles, jnp.int32))
    return moe_ffn(x, sched, (w1[None], w3[None]), out_dtype=BF16, tm=tm, tn=tn, name="swiglu_up")


def _merge_kernel(ya_ref, yb_ref, yc_ref, yd_ref, g0_ref, g1_ref, g2_ref, g3_ref, wb_ref, o_ref, wbf_ref):
    @pl.when(pl.program_id(1) == 0)
    def _():
        wbf_ref[...] = wb_ref[...].astype(BF16)

    pairs = ((ya_ref, g0_ref), (yb_ref, g1_ref), (yc_ref, g2_ref), (yd_ref, g3_ref))
    for c0, cn in _col_chunks(o_ref.shape[1]):
        acc = None
        for g, (y_ref, g_ref) in enumerate(pairs):
            proj = jnp.dot(y_ref[...], wbf_ref[g, :, c0:c0 + cn], preferred_element_type=F32)
            term = g_ref[:, c0:c0 + cn].astype(F32) * proj
            acc = term if acc is None else acc + term
        o_ref[:, c0:c0 + cn] = acc.astype(o_ref.dtype)


def merge_branches(branches, gates, w_branch, *, tm=1024, tn=512):
    t, bw = branches[0].shape
    d = w_branch.shape[2]
    tm = _pick(t, tm)
    tn = _pick(d, tn)
    nj = d // tn
    y_spec = pl.BlockSpec((tm, bw), lambda j, i: (i, 0))
    gate_specs = [pl.BlockSpec((tm, tn), functools.partial(lambda j, i, g: (i, g * nj + j), g=g))
                  for g in range(N_BRANCH)]
    nb = (2 * N_BRANCH * _nbytes((tm, bw), BF16) + 2 * N_BRANCH * _nbytes((tm, tn), BF16)
          + 2 * _nbytes((N_BRANCH, bw, tn), F32) + _nbytes((N_BRANCH, bw, tn), BF16)
          + 2 * _nbytes((tm, tn), BF16) + 3 * _nbytes((tm, tn), F32))
    return pl.pallas_call(
        _merge_kernel,
        grid=(nj, t // tm),
        in_specs=[y_spec] * N_BRANCH + gate_specs + [pl.BlockSpec((N_BRANCH, bw, tn), lambda j, i: (0, 0, j))],
        out_specs=pl.BlockSpec((tm, tn), lambda j, i: (i, j)),
        out_shape=jax.ShapeDtypeStruct((t, d), BF16),
        scratch_shapes=[pltpu.VMEM((N_BRANCH, bw, tn), BF16)],
        compiler_params=_params(("arbitrary", "arbitrary"), nb),
        name="merge_branches",
    )(*branches, gates, gates, gates, gates, w_branch)


def _halo_specs(ts, width, col, seq_tiles_total):
    r = ts // HALO
    last = seq_tiles_total * r - 1
    prev = pl.BlockSpec((HALO, width), lambda i: (jnp.maximum(i * r - 1, 0), col))
    cur = pl.BlockSpec((ts, width), lambda i: (i, col))
    nxt = pl.BlockSpec((HALO, width), lambda i: (jnp.minimum((i + 1) * r, last), col))
    return [prev, cur, nxt]


def _edge_flags(ts, seq):
    tiles_per_seq = seq // ts
    k = pl.program_id(0) % tiles_per_seq
    return k == 0, k == tiles_per_seq - 1, k * ts


SUBLANES = 8


def _conv_kernel(lp_ref, lc_ref, ln_ref, gp_ref, gc_ref, gn_ref, cw_ref, cb_ref, lg_ref, lb_ref, o_ref,
                 abuf_ref, ash_ref, ybuf_ref, *, ts, seq):
    first, last, _ = _edge_flags(ts, seq)
    width = lc_ref.shape[1]
    glu = lambda l_ref, g_ref: l_ref[...].astype(F32) * jax.nn.sigmoid(g_ref[...].astype(F32))
    abuf_ref[0:HALO, :] = jnp.where(first, 0.0, glu(lp_ref, gp_ref))
    abuf_ref[HALO + ts:2 * HALO + ts, :] = jnp.where(last, 0.0, glu(ln_ref, gn_ref))
    abuf_ref[HALO:HALO + ts, :] = glu(lc_ref, gc_ref)

    span = ash_ref.shape[1]
    for p in range(1, SUBLANES):
        ash_ref[p - 1] = abuf_ref[p:p + span, :]

    rows = 64 if ts % 64 == 0 else ts
    for r0 in range(0, ts, rows):
        for c0 in range(0, width, 128):
            acc = jnp.zeros((rows, 128), F32)
            for k in range(CONV_WIDTH):
                q, p = divmod(HALO - CONV_PAD + k, SUBLANES)
                start = r0 + q * SUBLANES
                src = abuf_ref if p == 0 else ash_ref.at[p - 1]
                acc = acc + cw_ref[k:k + 1, c0:c0 + 128] * src[start:start + rows, c0:c0 + 128]
            ybuf_ref[r0:r0 + rows, c0:c0 + 128] = acc + cb_ref[:, c0:c0 + 128]

    y = ybuf_ref[...]
    mu = jnp.mean(y, axis=-1, keepdims=True)
    yc = y - mu
    var = jnp.mean(yc * yc, axis=-1, keepdims=True)
    z = yc * lax.rsqrt(var + EPS) * lg_ref[...] + lb_ref[...]
    o_ref[...] = jax.nn.silu(z).astype(o_ref.dtype)


def conv_branch(hin, conv_w, conv_b, ln_g, ln_b, *, seq, bw, ts=256):
    t = hin.shape[0]
    ts = _pick(seq, ts)
    n_tiles = t // ts
    row = lambda v: v.reshape(1, bw)
    const = lambda shape: pl.BlockSpec(shape, lambda i: (0, 0))
    span = ts + 2 * HALO - SUBLANES
    nb = (4 * _nbytes((ts + 2 * HALO, bw), BF16) + 2 * _nbytes((ts, bw), BF16) + _nbytes((ts + 2 * HALO, bw), F32)
          + (SUBLANES - 1) * _nbytes((span, bw), F32) + 5 * _nbytes((ts, bw), F32))
    return pl.pallas_call(
        functools.partial(_conv_kernel, ts=ts, seq=seq),
        grid=(n_tiles,),
        in_specs=_halo_specs(ts, bw, 0, n_tiles) + _halo_specs(ts, bw, 1, n_tiles)
        + [const((CONV_WIDTH, bw)), const((1, bw)), const((1, bw)), const((1, bw))],
        out_specs=pl.BlockSpec((ts, bw), lambda i: (i, 0)),
        out_shape=jax.ShapeDtypeStruct((t, bw), BF16),
        scratch_shapes=[pltpu.VMEM((ts + 2 * HALO, bw), F32), pltpu.VMEM((SUBLANES - 1, span, bw), F32),
                        pltpu.VMEM((ts, bw), F32)],
        compiler_params=_params(("parallel",), nb),
        name="conv_branch",
    )(hin, hin, hin, hin, hin, hin, conv_w, row(conv_b), row(ln_g), row(ln_b))


def _pool_kernel(pp_ref, pc_ref, pn_ref, pw_ref, ps_ref, o_ref, pbuf_ref, *, ts, seq):
    first, last, pos0 = _edge_flags(ts, seq)
    n_g = len(POOL_WINDOWS)
    pg = pc_ref.shape[1] // n_g
    pbuf_ref[0:HALO, :] = jnp.where(first, 0.0, pp_ref[...].astype(F32))
    pbuf_ref[HALO + ts:2 * HALO + ts, :] = jnp.where(last, 0.0, pn_ref[...].astype(F32))
    pbuf_ref[HALO:HALO + ts, :] = pc_ref[...].astype(F32)
    pos = pos0 + lax.broadcasted_iota(jnp.int32, (ts, 1), 0)
    for g, w in enumerate(POOL_WINDOWS):
        cols = slice(g * pg, (g + 1) * pg)
        win = None
        for d in range(-(w // 2), w // 2):
            term = pbuf_ref[HALO + d:HALO + d + ts, cols]
            win = term if win is None else win + term
        lo = jnp.maximum(pos - w // 2, 0)
        hi = jnp.minimum(pos + w // 2 - 1, seq - 1)
        cnt = (hi - lo + 1).astype(F32)
        pooled = win / cnt - pbuf_ref[HALO:HALO + ts, cols]
        mixed = jnp.dot(pooled.astype(BF16), pw_ref[g].astype(BF16), preferred_element_type=F32)
        o_ref[:, cols] = (mixed * ps_ref[:, cols]).astype(o_ref.dtype)


def pool_branch(hin, pool_w, pool_scale, *, seq, bw, col, ts=256):
    t = hin.shape[0]
    ts = _pick(seq, ts)
    n_tiles = t // ts
    nb = 2 * _nbytes((ts + 2 * HALO, bw), F32) + 2 * _nbytes((ts, bw), BF16) + 4 * _nbytes((ts + 2 * HALO, bw), F32)
    nb += 2 * _nbytes(pool_w.shape, F32)
    return pl.pallas_call(
        functools.partial(_pool_kernel, ts=ts, seq=seq),
        grid=(n_tiles,),
        in_specs=_halo_specs(ts, bw, col, n_tiles)
        + [pl.BlockSpec(pool_w.shape, lambda i: (0, 0, 0)), pl.BlockSpec((1, bw), lambda i: (0, 0))],
        out_specs=pl.BlockSpec((ts, bw), lambda i: (i, 0)),
        out_shape=jax.ShapeDtypeStruct((t, bw), BF16),
        scratch_shapes=[pltpu.VMEM((ts + 2 * HALO, bw), F32)],
        compiler_params=_params(("parallel",), nb),
        name="pool_branch",
    )(hin, hin, hin, pool_w, pool_scale.reshape(1, bw))


def _dft_tables(seq, group):
    c = np.arange(group)
    ang = 2.0 * np.pi * np.outer(c, c) / group
    scale = 1.0 / np.sqrt(float(seq) * group)
    chan = np.concatenate([np.cos(ang), np.sin(ang)], axis=1) * scale
    fa = 64 if seq % 64 == 0 else 1
    sp = np.arange(seq)[:, None]
    ang_a = 2.0 * np.pi * ((sp * fa * np.arange(seq // fa)[None, :]) % seq) / seq
    ang_b = 2.0 * np.pi * ((sp * np.arange(fa)[None, :]) % seq) / seq
    ca, sa = jnp.asarray(np.cos(ang_a), F32)[:, :, None], jnp.asarray(np.sin(ang_a), F32)[:, :, None]
    cb, sb = jnp.asarray(np.cos(ang_b), F32)[:, None, :], jnp.asarray(np.sin(ang_b), F32)[:, None, :]
    cos_s = (ca * cb - sa * sb).reshape(seq, seq)
    sin_s = (sa * cb + ca * sb).reshape(seq, seq)
    pos = jnp.concatenate([cos_s, -sin_s], axis=1).astype(BF16)
    return jnp.asarray(chan, F32).astype(BF16), pos


def _chan_dft_kernel(f_ref, tbl_ref, o_ref):
    n_g = N_FFT_GROUPS
    fg = f_ref.shape[1] // n_g
    tbl = tbl_ref[...]
    for g in range(n_g):
        r = jnp.dot(f_ref[:, g * fg:(g + 1) * fg].astype(BF16), tbl, preferred_element_type=F32)
        o_ref[0, :, g * fg:(g + 1) * fg] = r[:, :fg].astype(o_ref.dtype)
        o_ref[1, :, g * fg:(g + 1) * fg] = r[:, fg:].astype(o_ref.dtype)


def fourier_branch(hin, chan_tbl, pos_tbl, *, batch, seq, bw, col, ts=512):
    t = hin.shape[0]
    ts = _pick(seq, ts)
    tiles_per_seq = seq // ts
    nb = 2 * _nbytes((ts, bw), F32) + 4 * _nbytes((ts, bw), BF16) + 2 * _nbytes(chan_tbl.shape, BF16)
    nb += 4 * _nbytes((ts, bw), F32)
    proj = pl.pallas_call(
        _chan_dft_kernel,
        grid=(t // ts,),
        in_specs=[pl.BlockSpec((ts, bw), lambda i: (i, col)), pl.BlockSpec(chan_tbl.shape, lambda i: (0, 0))],
        out_specs=pl.BlockSpec((None, 2, ts, bw), lambda i: (i // tiles_per_seq, 0, i % tiles_per_seq, 0)),
        out_shape=jax.ShapeDtypeStruct((batch, 2, seq, bw), BF16),
        compiler_params=_params(("parallel",), nb),
        name="fourier_channels",
    )(hin, chan_tbl)
    proj = proj.reshape(batch, 2 * seq, bw)
    outs = [matmul(pos_tbl, proj, out_dtype=BF16, tm=512, tn=512, w_lead=b) for b in range(batch)]
    return jnp.concatenate(outs, axis=0)


def _gmlp_kernel(u_ref, v_ref, lg_ref, lb_ref, ws_ref, bias_ref, o_ref, *, ts):
    v = v_ref[...].astype(F32)
    mu = jnp.mean(v, axis=-1, keepdims=True)
    vc = v - mu
    var = jnp.mean(vc * vc, axis=-1, keepdims=True)
    vn = (vc * lax.rsqrt(var + EPS) * lg_ref[...] + lb_ref[...]).astype(BF16)
    n_chunks = ts // GMLP_CHUNK
    n_heads = v.shape[1] // GMLP_HEAD_DIM
    for h in range(n_heads):
        cols = slice(h * GMLP_HEAD_DIM, (h + 1) * GMLP_HEAD_DIM)
        rhs = jnp.concatenate([vn[c * GMLP_CHUNK:(c + 1) * GMLP_CHUNK, cols] for c in range(n_chunks)], axis=1)
        s = jnp.dot(ws_ref[h].astype(BF16), rhs, preferred_element_type=F32)
        for c in range(n_chunks):
            rows = slice(c * GMLP_CHUNK, (c + 1) * GMLP_CHUNK)
            sc = s[:, c * GMLP_HEAD_DIM:(c + 1) * GMLP_HEAD_DIM] + bias_ref[:, cols]
            o_ref[rows, cols] = (u_ref[rows, cols].astype(F32) * sc).astype(o_ref.dtype)


def gmlp_branch(hin, ln_g, ln_b, ws, bias, *, bw, col_u, col_v, ts=512):
    t = hin.shape[0]
    ts = _pick(t, ts)
    assert ts % GMLP_CHUNK == 0
    n_heads = bw // GMLP_HEAD_DIM
    bias_rows = jnp.repeat(bias.T, GMLP_HEAD_DIM, axis=1)
    nb = 4 * _nbytes((ts, bw), F32) + 2 * _nbytes((ts, bw), BF16) + 6 * _nbytes((ts, bw), F32)
    return pl.pallas_call(
        functools.partial(_gmlp_kernel, ts=ts),
        grid=(t // ts,),
        in_specs=[pl.BlockSpec((ts, bw), lambda i: (i, col_u)), pl.BlockSpec((ts, bw), lambda i: (i, col_v)),
                  pl.BlockSpec((1, bw), lambda i: (0, 0)), pl.BlockSpec((1, bw), lambda i: (0, 0)),
                  pl.BlockSpec((n_heads, GMLP_CHUNK, GMLP_CHUNK), lambda i: (0, 0, 0)),
                  pl.BlockSpec((GMLP_CHUNK, bw), lambda i: (0, 0))],
        out_specs=pl.BlockSpec((ts, bw), lambda i: (i, 0)),
        out_shape=jax.ShapeDtypeStruct((t, bw), BF16),
        compiler_params=_params(("parallel",), nb),
        name="gmlp_branch",
    )(hin, hin, ln_g.reshape(1, bw), ln_b.reshape(1, bw), ws, bias_rows)


def _route(idx, tm):
    a = idx.size
    n_tiles = a // tm + N_EXPERTS
    experts = jnp.arange(N_EXPERTS, dtype=jnp.int32)
    e_flat = idx.reshape(a)
    onehot = (e_flat[:, None] == experts[None, :]).astype(jnp.int32)
    csum = jnp.cumsum(onehot, axis=0)
    counts = csum[-1]
    rank = jnp.sum((csum - onehot) * onehot, axis=1)
    tiles_e = (counts + tm - 1) // tm
    tile_end = jnp.cumsum(tiles_e)
    tile_start = tile_end - tiles_e
    pos = jnp.sum(onehot * tile_start[None, :], axis=1) * tm + rank
    src_tok = jnp.zeros((n_tiles * tm,), jnp.int32).at[pos].set(jnp.arange(a, dtype=jnp.int32) // TOP_K)
    n_used = tile_end[-1]
    tile_ids = jnp.minimum(jnp.arange(n_tiles, dtype=jnp.int32), n_used - 1)
    tile_expert = jnp.sum((tile_ids[:, None] >= tile_end[None, :]).astype(jnp.int32), axis=1)
    first = jnp.concatenate([jnp.ones((1,), jnp.int32), (tile_expert[1:] != tile_expert[:-1]).astype(jnp.int32)])
    present = tiles_e > 0
    first_present = jnp.min(jnp.where(present, experts, N_EXPERTS))
    last_present = jnp.max(jnp.where(present, experts, -1))
    later = jnp.logical_and(present[None, :], experts[None, :] > experts[:, None])
    next_present = jnp.min(jnp.where(later, experts[None, :], N_EXPERTS), axis=1)
    next_present = jnp.where(next_present == N_EXPERTS, first_present, next_present)
    nxt = next_present[tile_expert]
    last_group = (tile_expert == last_present).astype(jnp.int32)
    rows_valid = counts[tile_expert] - (tile_ids - tile_start[tile_expert]) * tm
    full = (rows_valid > tm // 2).astype(jnp.int32)
    sched = (tile_expert, first, nxt, last_group, full, n_used.reshape(1).astype(jnp.int32))
    return pos, src_tok, sched


def _row_copy(src_hbm, row, dst_ref, r, sem):
    return pltpu.make_async_copy(src_hbm.at[pl.ds(row, 1), :], dst_ref.at[pl.ds(r, 1), :], sem)


ROW_DMA_UNROLL = 8
DMA_PRIORITIES = 2


def _gather_rows_kernel(tok_ref, nrows_ref, src_hbm, o_ref, buf_ref, sem, *, tg, n_steps):
    i = pl.program_id(0)

    def start_step(step):
        slot = step % 2

        def issue(pair, c):
            for prio in range(DMA_PRIORITIES):
                r = pair * DMA_PRIORITIES + prio
                _row_copy(src_hbm, tok_ref[step * tg + r], buf_ref.at[slot], r, sem.at[slot]).start(priority=prio)
            return c

        lax.fori_loop(0, tg // DMA_PRIORITIES, issue, 0, unroll=ROW_DMA_UNROLL // DMA_PRIORITIES)

    def active(step):
        return step * tg < nrows_ref[0]

    @pl.when(i == 0)
    def _():
        start_step(i)

    @pl.when(jnp.logical_and(i + 1 < n_steps, active(i + 1)))
    def _():
        start_step(i + 1)

    @pl.when(active(i))
    def _():
        slot = i % 2

        def drain(r, c):
            _row_copy(src_hbm, 0, buf_ref.at[slot], r, sem.at[slot]).wait()
            return c

        lax.fori_loop(0, tg, drain, 0, unroll=ROW_DMA_UNROLL)
        o_ref[...] = buf_ref[slot].astype(o_ref.dtype)

    @pl.when(jnp.logical_not(active(i)))
    def _():
        o_ref[...] = jnp.zeros_like(o_ref)


def gather_rows(src, tok, n_rows, *, out_dtype, tg=256):
    p = tok.shape[0]
    d = src.shape[1]
    tg = _pick(p, tg)
    n_steps = p // tg
    nb = 2 * _nbytes((tg, d), F32) + 2 * _nbytes((tg, d), out_dtype) + _nbytes((tg, d), F32)
    return pl.pallas_call(
        functools.partial(_gather_rows_kernel, tg=tg, n_steps=n_steps),
        grid_spec=pltpu.PrefetchScalarGridSpec(
            num_scalar_prefetch=2,
            grid=(n_steps,),
            in_specs=[pl.BlockSpec(memory_space=pl.ANY)],
            out_specs=pl.BlockSpec((tg, d), lambda i, tok, nrows: (i, 0)),
            scratch_shapes=[pltpu.VMEM((2, tg, d), F32), pltpu.SemaphoreType.DMA((2,))],
        ),
        out_shape=jax.ShapeDtypeStruct((p, d), out_dtype),
        compiler_params=_params(("arbitrary",), nb),
        name="gather_rows",
    )(tok, n_rows, src)


def _moe_ffn_kernel(te_ref, first_ref, nxt_ref, lastg_ref, full_ref, nu_ref, x_ref, *refs, n_w, tn, nj, tm):
    w_hbm = refs[:n_w]
    o_ref = refs[n_w]
    stage = refs[n_w + 1:2 * n_w + 1]
    wbf = refs[2 * n_w + 1:3 * n_w + 1]
    sem = refs[3 * n_w + 1]
    j = pl.program_id(0)
    n = pl.program_id(1)
    used = n < nu_ref[0]

    def copies(e, jj):
        cols = pl.ds(pl.multiple_of(jj * tn, tn), tn)
        return [pltpu.make_async_copy(w_hbm[q].at[e, :, cols], stage[q], sem.at[q]) for q in range(n_w)]

    @pl.when(jnp.logical_and(used, first_ref[n] == 1))
    def _():
        @pl.when(jnp.logical_and(j == 0, n == 0))
        def _():
            for c in copies(te_ref[n], j):
                c.start()

        for c in copies(te_ref[n], j):
            c.wait()
        _cast_staged(stage, wbf)
        is_last = lastg_ref[n] == 1

        @pl.when(jnp.logical_not(is_last))
        def _():
            for c in copies(nxt_ref[n], j):
                c.start()

        @pl.when(jnp.logical_and(is_last, j + 1 < nj))
        def _():
            for c in copies(nxt_ref[n], j + 1):
                c.start()

    def compute(rows):
        x = x_ref[0:rows, :]
        acc = jnp.dot(x, wbf[0][...], preferred_element_type=F32)
        if n_w == 2:
            acc = jax.nn.silu(acc) * jnp.dot(x, wbf[1][...], preferred_element_type=F32)
        o_ref[0:rows, :] = acc.astype(o_ref.dtype)

    is_full = full_ref[n] == 1

    @pl.when(jnp.logical_and(used, is_full))
    def _():
        compute(tm)

    @pl.when(jnp.logical_and(used, jnp.logical_not(is_full)))
    def _():
        compute(tm // 2)
        o_ref[tm // 2:tm, :] = jnp.zeros((tm - tm // 2, tn), o_ref.dtype)

    @pl.when(jnp.logical_not(used))
    def _():
        o_ref[...] = jnp.zeros_like(o_ref)


def moe_ffn(xs, sched, weights, *, out_dtype, tm, tn=512, name="moe_ffn"):
    p, k = xs.shape
    n_w = len(weights)
    n_out = weights[0].shape[2]
    tn = _pick(n_out, tn)
    nj = n_out // tn
    x_spec = pl.BlockSpec((tm, k), lambda j, n, te, fi, nx, lg, fu, nu: (jnp.minimum(n, nu[0] - 1), 0))
    o_spec = pl.BlockSpec((tm, tn), lambda j, n, te, fi, nx, lg, fu, nu: (n, j))
    nb = (2 * _nbytes((tm, k), BF16) + n_w * _nbytes((k, tn), F32) + n_w * _nbytes((k, tn), BF16)
          + 2 * _nbytes((tm, tn), out_dtype) + (n_w + 1) * _nbytes((tm, tn), F32))
    scratch = ([pltpu.VMEM((k, tn), F32)] * n_w + [pltpu.VMEM((k, tn), BF16)] * n_w
               + [pltpu.SemaphoreType.DMA((n_w,))])
    return pl.pallas_call(
        functools.partial(_moe_ffn_kernel, n_w=n_w, tn=tn, nj=nj, tm=tm),
        grid_spec=pltpu.PrefetchScalarGridSpec(
            num_scalar_prefetch=6,
            grid=(nj, p // tm),
            in_specs=[x_spec] + [pl.BlockSpec(memory_space=pl.ANY)] * n_w,
            out_specs=o_spec,
            scratch_shapes=scratch,
        ),
        out_shape=jax.ShapeDtypeStruct((p, n_out), out_dtype),
        compiler_params=_params(("arbitrary", "arbitrary"), nb),
        name=name,
    )(*sched, xs, *weights)


def _combine_norm_kernel(pos_ref, h_ref, wts_ref, ys_hbm, g_ref, o_ref, buf_ref, sem, *, tg, n_steps):
    i = pl.program_id(0)

    def start_step(step):
        slot = step % 2

        def issue(r, c):
            for k in range(TOP_K):
                row = pos_ref[(step * tg + r) * TOP_K + k]
                _row_copy(ys_hbm, row, buf_ref.at[slot, k], r, sem.at[slot]).start(priority=k % DMA_PRIORITIES)
            return c

        lax.fori_loop(0, tg, issue, 0, unroll=ROW_DMA_UNROLL)

    @pl.when(i == 0)
    def _():
        start_step(i)

    @pl.when(i + 1 < n_steps)
    def _():
        start_step(i + 1)

    slot = i % 2

    def drain(r, c):
        for k in range(TOP_K):
            _row_copy(ys_hbm, 0, buf_ref.at[slot, k], r, sem.at[slot]).wait()
        return c

    lax.fori_loop(0, tg, drain, 0, unroll=ROW_DMA_UNROLL)
    h = h_ref[...]
    for k in range(TOP_K):
        h = h + wts_ref[:, k:k + 1] * buf_ref[slot, k]
    o_ref[...] = _rmsnorm_rows(h, g_ref[...]).astype(o_ref.dtype)


def combine_norm(h, wts, pos, ys, g, *, tg=128):
    t, d = h.shape
    tg = _pick(t, tg)
    n_steps = t // tg
    nb = 4 * _nbytes((tg, d), F32) + 2 * TOP_K * _nbytes((tg, d), F32) + 4 * _nbytes((tg, d), F32)
    return pl.pallas_call(
        functools.partial(_combine_norm_kernel, tg=tg, n_steps=n_steps),
        grid_spec=pltpu.PrefetchScalarGridSpec(
            num_scalar_prefetch=1,
            grid=(n_steps,),
            in_specs=[pl.BlockSpec((tg, d), lambda i, pos: (i, 0)), pl.BlockSpec((tg, TOP_K), lambda i, pos: (i, 0)),
                      pl.BlockSpec(memory_space=pl.ANY), pl.BlockSpec((1, d), lambda i, pos: (0, 0))],
            out_specs=pl.BlockSpec((tg, d), lambda i, pos: (i, 0)),
            scratch_shapes=[pltpu.VMEM((2, TOP_K, tg, d), F32), pltpu.SemaphoreType.DMA((2,))],
        ),
        out_shape=jax.ShapeDtypeStruct((t, d), F32),
        compiler_params=_params(("arbitrary",), nb),
        name="combine_norm",
    )(pos, h, wts, ys, g.reshape(1, d))


def moe_ffn_then_norm(h, norm_g, router, w1, w3, w2, final_g, *, tm=512):
    t = h.shape[0]
    tm = _pick(t * TOP_K, tm)
    hn, idx, wts = rmsnorm_router(h, norm_g, router)
    pos, src_tok, sched = _route(idx, tm)
    xs = gather_rows(hn, src_tok, sched[-1] * tm, out_dtype=BF16)
    acts = moe_ffn(xs, sched, (w1, w3), out_dtype=BF16, tm=tm, name="moe_up")
    ys = moe_ffn(acts, sched, (w2,), out_dtype=F32, tm=tm, tn=1024, name="moe_down")
    return combine_norm(h, wts, pos, ys, final_g)


def _mixer(h, norm1, w_in, w_gate, conv_w, conv_b, conv_ln_g, conv_ln_b, pool_w, pool_scale, gmlp_ln_g, gmlp_ln_b,
           gmlp_ws, gmlp_b, w_branch, w_out, tables, *, batch, seq):
    bw = w_branch.shape[1]
    xn = rmsnorm(h, norm1, BF16)
    hin = matmul(xn, w_in, out_dtype=BF16)
    ya = conv_branch(hin, conv_w, conv_b, conv_ln_g, conv_ln_b, seq=seq, bw=bw)
    yb = pool_branch(hin, pool_w, pool_scale, seq=seq, bw=bw, col=2)
    yc = fourier_branch(hin, *tables, batch=batch, seq=seq, bw=bw, col=3)
    yd = gmlp_branch(hin, gmlp_ln_g, gmlp_ln_b, gmlp_ws, gmlp_b, bw=bw, col_u=4, col_v=5)
    gates = matmul(xn, w_gate, out_dtype=BF16, epilogue="sigmoid")
    merged = merge_branches((ya, yb, yc, yd), gates, w_branch)
    return matmul(merged, w_out, out_dtype=F32, res=h, tm=512)


def kernel(x, l0_norm1, l0_w_in, l0_w_gate, l0_conv_w, l0_conv_b, l0_conv_ln_g, l0_conv_ln_b, l0_pool_w, l0_pool_scale, l0_gmlp_ln_g, l0_gmlp_ln_b, l0_gmlp_ws, l0_gmlp_b, l0_w_branch, l0_w_out, l0_norm2, l0_ffn_w1, l0_ffn_w3, l0_ffn_w2, l1_norm1, l1_w_in, l1_w_gate, l1_conv_w, l1_conv_b, l1_conv_ln_g, l1_conv_ln_b, l1_pool_w, l1_pool_scale, l1_gmlp_ln_g, l1_gmlp_ln_b, l1_gmlp_ws, l1_gmlp_b, l1_w_branch, l1_w_out, l1_norm2, l1_router, l1_exp_w1, l1_exp_w3, l1_exp_w2, final_norm):
    batch, seq, d = x.shape
    bw = l0_w_branch.shape[1]
    tables = _dft_tables(seq, bw // N_FFT_GROUPS)
    h = x.reshape(batch * seq, d)

    h = _mixer(h, l0_norm1, l0_w_in, l0_w_gate, l0_conv_w, l0_conv_b, l0_conv_ln_g, l0_conv_ln_b, l0_pool_w,
               l0_pool_scale, l0_gmlp_ln_g, l0_gmlp_ln_b, l0_gmlp_ws, l0_gmlp_b, l0_w_branch, l0_w_out, tables,
               batch=batch, seq=seq)
    hn = rmsnorm(h, l0_norm2, BF16)
    acts = swiglu_up(hn, l0_ffn_w1, l0_ffn_w3)
    half = l0_ffn_w2.shape[0] // 2
    h = matmul(acts, l0_ffn_w2, out_dtype=F32, res=h, tn=512, k_block=(0, half))
    h = matmul(acts, l0_ffn_w2, out_dtype=F32, res=h, tn=512, k_block=(1, half))

    h = _mixer(h, l1_norm1, l1_w_in, l1_w_gate, l1_conv_w, l1_conv_b, l1_conv_ln_g, l1_conv_ln_b, l1_pool_w,
               l1_pool_scale, l1_gmlp_ln_g, l1_gmlp_ln_b, l1_gmlp_ws, l1_gmlp_b, l1_w_branch, l1_w_out, tables,
               batch=batch, seq=seq)
    out = moe_ffn_then_norm(h, l1_norm2, l1_router, l1_exp_w1, l1_exp_w3, l1_exp_w2, final_norm)
    return out.reshape(batch, seq, d)
```

```python
import functools

import numpy as np
import jax
import jax.numpy as jnp
from jax import lax
from jax.experimental import pallas as pl
from jax.experimental.pallas import tpu as pltpu

F32 = jnp.float32
BF16 = jnp.bfloat16

N_BRANCH = 4
CONV_WIDTH = 31
CONV_PAD = CONV_WIDTH // 2
POOL_WINDOWS = (2, 4, 8, 16)
N_FFT_GROUPS = 4
GMLP_CHUNK = 128
GMLP_HEAD_DIM = 128
N_EXPERTS = 8
TOP_K = 2
EPS = 1e-6

HALO = 16
V7X_VMEM_BYTES = 64 * 1024 * 1024
VMEM_CAP = V7X_VMEM_BYTES - 6 * 1024 * 1024


def _vmem_limit(nbytes):
    return int(min(VMEM_CAP, nbytes + nbytes // 4 + (4 << 20)))


def _nbytes(shape, dtype):
    return int(np.prod(shape)) * jnp.dtype(dtype).itemsize


def _params(sem, nbytes):
    return pltpu.CompilerParams(dimension_semantics=sem, vmem_limit_bytes=_vmem_limit(nbytes))


def _pick(n, pref):
    t = min(n, pref)
    while n % t:
        t -= 8
    return t


def _rmsnorm_rows(x, g):
    return x * lax.rsqrt(jnp.mean(x * x, axis=-1, keepdims=True) + EPS) * g


def _rmsnorm_kernel(x_ref, g_ref, o_ref):
    o_ref[...] = _rmsnorm_rows(x_ref[...], g_ref[...]).astype(o_ref.dtype)


def rmsnorm(x, g, out_dtype):
    t, d = x.shape
    tm = _pick(t, 256)
    nb = 2 * _nbytes((tm, d), F32) + 2 * _nbytes((tm, d), out_dtype)
    return pl.pallas_call(
        _rmsnorm_kernel,
        grid=(t // tm,),
        in_specs=[pl.BlockSpec((tm, d), lambda i: (i, 0)), pl.BlockSpec((1, d), lambda i: (0, 0))],
        out_specs=pl.BlockSpec((tm, d), lambda i: (i, 0)),
        out_shape=jax.ShapeDtypeStruct((t, d), out_dtype),
        compiler_params=_params(("parallel",), nb),
        name="rmsnorm",
    )(x, g.reshape(1, d))


def _rmsnorm_router_kernel(x_ref, g_ref, r_ref, o_ref, idx_ref, wts_ref):
    y = _rmsnorm_rows(x_ref[...], g_ref[...])
    o_ref[...] = y
    n_e = r_ref.shape[1] // 2
    y_hi = y.astype(BF16)
    y_lo = (y - y_hi.astype(F32)).astype(BF16)
    p_hi = jnp.dot(y_hi, r_ref[...], preferred_element_type=F32)
    p_lo = jnp.dot(y_lo, r_ref[...], preferred_element_type=F32)
    logits = p_hi[:, :n_e] + (p_hi[:, n_e:] + p_lo[:, :n_e])
    lane = lax.broadcasted_iota(jnp.int32, logits.shape, 1)
    m1 = jnp.max(logits, axis=-1, keepdims=True)
    i1 = jnp.min(jnp.where(logits == m1, lane, n_e), axis=-1, keepdims=True)
    rest = jnp.where(lane == i1, -jnp.inf, logits)
    m2 = jnp.max(rest, axis=-1, keepdims=True)
    i2 = jnp.min(jnp.where(rest == m2, lane, n_e), axis=-1, keepdims=True)
    e2 = jnp.exp(m2 - m1)
    den = 1.0 + e2
    slot = lax.broadcasted_iota(jnp.int32, idx_ref.shape, 1)
    idx_ref[...] = jnp.where(slot == 0, i1, i2)
    wts_ref[...] = jnp.where(slot == 0, 1.0 / den, e2 / den)


def rmsnorm_router(x, g, router):
    t, d = x.shape
    n_e = router.shape[1]
    tm = _pick(t, 256)
    r_hi = router.astype(BF16)
    r_lo = (router - r_hi.astype(F32)).astype(BF16)
    r_parts = jnp.concatenate([r_hi, r_lo], axis=1)
    nb = 6 * _nbytes((tm, d), F32) + 2 * _nbytes((d, 128), BF16)
    return pl.pallas_call(
        _rmsnorm_router_kernel,
        grid=(t // tm,),
        in_specs=[pl.BlockSpec((tm, d), lambda i: (i, 0)), pl.BlockSpec((1, d), lambda i: (0, 0)),
                  pl.BlockSpec((d, 2 * n_e), lambda i: (0, 0))],
        out_specs=[pl.BlockSpec((tm, d), lambda i: (i, 0)), pl.BlockSpec((tm, TOP_K), lambda i: (i, 0)),
                   pl.BlockSpec((tm, TOP_K), lambda i: (i, 0))],
        out_shape=[jax.ShapeDtypeStruct((t, d), F32), jax.ShapeDtypeStruct((t, TOP_K), jnp.int32),
                   jax.ShapeDtypeStruct((t, TOP_K), F32)],
        compiler_params=_params(("parallel",), nb),
        name="rmsnorm_router",
    )(x, g.reshape(1, d), r_parts)


MXU_COLS = 256
CAST_ROWS = 128


def _col_chunks(tn):
    cn = MXU_COLS if tn % MXU_COLS == 0 else tn
    return [(c0, cn) for c0 in range(0, tn, cn)]


def _cast_staged(stage_refs, wbf_refs):
    k = stage_refs[0].shape[0]

    def body(c, carry):
        rows = pl.ds(pl.multiple_of(c * CAST_ROWS, CAST_ROWS), CAST_ROWS)
        for stage_ref, wbf_ref in zip(stage_refs, wbf_refs):
            wbf_ref[rows, :] = stage_ref[rows, :].astype(BF16)
        return carry

    lax.fori_loop(0, k // CAST_ROWS, body, 0)


def _matmul_kernel(*refs, staged, has_res, epilogue, tk, tn, nj, k0):
    x_ref, w_ref = refs[0], refs[1]
    pos = 2
    r_ref = None
    if has_res:
        r_ref = refs[pos]
        pos += 1
    o_ref = refs[pos]
    if staged:
        stage_ref, wbf_ref, sem = refs[pos + 1:pos + 4]
        j = pl.program_id(0)

        def copy(jj):
            cols = pl.ds(pl.multiple_of(jj * tn, tn), tn)
            return pltpu.make_async_copy(w_ref.at[pl.ds(k0, tk), cols], stage_ref, sem)

        @pl.when(pl.program_id(1) == 0)
        def _():
            @pl.when(j == 0)
            def _():
                copy(j).start()

            copy(j).wait()
            _cast_staged([stage_ref], [wbf_ref])

            @pl.when(j + 1 < nj)
            def _():
                copy(j + 1).start()

        w = wbf_ref[...]
    else:
        w = w_ref[...]
    acc = jnp.dot(x_ref[...], w, preferred_element_type=F32)
    if epilogue == "sigmoid":
        acc = jax.nn.sigmoid(acc)
    if has_res:
        acc = acc + r_ref[...]
    o_ref[...] = acc.astype(o_ref.dtype)


def matmul(x, w, *, out_dtype, res=None, epilogue=None, tm=1024, tn=1024, k_block=None):
    m, kx = x.shape
    kw, n = w.shape
    assert w.dtype == F32
    if k_block is None:
        assert kx == kw
        kk, tk = 0, kx
    else:
        kk, tk = k_block
    tm = _pick(m, tm)
    tn = _pick(n, tn)
    nj = n // tn
    nb = 2 * _nbytes((tm, tk), BF16) + 2 * _nbytes((tm, tn), out_dtype) + 2 * _nbytes((tm, tn), F32)
    scratch = [pltpu.VMEM((tk, tn), F32), pltpu.VMEM((tk, tn), BF16), pltpu.SemaphoreType.DMA(())]
    nb += _nbytes((tk, tn), F32) + _nbytes((tk, tn), BF16)
    in_specs = [pl.BlockSpec((tm, tk), lambda j, i: (i, kk)), pl.BlockSpec(memory_space=pl.ANY)]
    args = [x, w]
    if res is not None:
        in_specs.append(pl.BlockSpec((tm, tn), lambda j, i: (i, j)))
        args.append(res)
        nb += 2 * _nbytes((tm, tn), res.dtype)
    return pl.pallas_call(
        functools.partial(_matmul_kernel, staged=True, has_res=res is not None, epilogue=epilogue,
                          tk=tk, tn=tn, nj=nj, k0=kk * tk),
        grid=(nj, m // tm),
        in_specs=in_specs,
        out_specs=pl.BlockSpec((tm, tn), lambda j, i: (i, j)),
        out_shape=jax.ShapeDtypeStruct((m, n), out_dtype),
        scratch_shapes=scratch,
        compiler_params=_params(("arbitrary", "arbitrary"), nb),
        name="matmul",
    )(*args)


def swiglu_up(x, w1, w3, *, tm=2048, tn=256):
    tm = _pick(x.shape[0], tm)
    n_tiles = x.shape[0] // tm
    zeros = jnp.zeros((n_tiles,), jnp.int32)
    ones = jnp.ones((n_tiles,), jnp.int32)
    first = zeros.at[0].set(1)
    sched = (zeros, first, zeros, ones, ones, jnp.full((1,), n_tiles, jnp.int32))
    return moe_ffn(x, sched, (w1[None], w3[None]), out_dtype=BF16, tm=tm, tn=tn, name="swiglu_up")


def _merge_kernel(ya_ref, yb_ref, yc_ref, yd_ref, g0_ref, g1_ref, g2_ref, g3_ref, wb_ref, o_ref, wbf_ref):
    @pl.when(pl.program_id(1) == 0)
    def _():
        wbf_ref[...] = wb_ref[...].astype(BF16)

    pairs = ((ya_ref, g0_ref), (yb_ref, g1_ref), (yc_ref, g2_ref), (yd_ref, g3_ref))
    for c0, cn in _col_chunks(o_ref.shape[1]):
        acc = None
        for g, (y_ref, g_ref) in enumerate(pairs):
            proj = jnp.dot(y_ref[...], wbf_ref[g, :, c0:c0 + cn], preferred_element_type=F32)
            term = g_ref[:, c0:c0 + cn].astype(F32) * proj
            acc = term if acc is None else acc + term
        o_ref[:, c0:c0 + cn] = acc.astype(o_ref.dtype)


def merge_branches(branches, gates, w_branch, *, tm=1024, tn=512):
    t, bw = branches[0].shape
    d = w_branch.shape[2]
    tm = _pick(t, tm)
    tn = _pick(d, tn)
    nj = d // tn
    y_spec = pl.BlockSpec((tm, bw), lambda j, i: (i, 0))
    gate_specs = [pl.BlockSpec((tm, tn), functools.partial(lambda j, i, g: (i, g * nj + j), g=g))
                  for g in range(N_BRANCH)]
    nb = (2 * N_BRANCH * _nbytes((tm, bw), BF16) + 2 * N_BRANCH * _nbytes((tm, tn), BF16)
          + 2 * _nbytes((N_BRANCH, bw, tn), F32) + _nbytes((N_BRANCH, bw, tn), BF16)
          + 2 * _nbytes((tm, tn), BF16) + 3 * _nbytes((tm, tn), F32))
    return pl.pallas_call(
        _merge_kernel,
        grid=(nj, t // tm),
        in_specs=[y_spec] * N_BRANCH + gate_specs + [pl.BlockSpec((N_BRANCH, bw, tn), lambda j, i: (0, 0, j))],
        out_specs=pl.BlockSpec((tm, tn), lambda j, i: (i, j)),
        out_shape=jax.ShapeDtypeStruct((t, d), BF16),
        scratch_shapes=[pltpu.VMEM((N_BRANCH, bw, tn), BF16)],
        compiler_params=_params(("arbitrary", "arbitrary"), nb),
        name="merge_branches",
    )(*branches, gates, gates, gates, gates, w_branch)


def _halo_specs(ts, width, col, seq_tiles_total):
    r = ts // HALO
    last = seq_tiles_total * r - 1
    prev = pl.BlockSpec((HALO, width), lambda i: (jnp.maximum(i * r - 1, 0), col))
    cur = pl.BlockSpec((ts, width), lambda i: (i, col))
    nxt = pl.BlockSpec((HALO, width), lambda i: (jnp.minimum((i + 1) * r, last), col))
    return [prev, cur, nxt]


def _edge_flags(ts, seq):
    tiles_per_seq = seq // ts
    k = pl.program_id(0) % tiles_per_seq
    return k == 0, k == tiles_per_seq - 1, k * ts


SUBLANES = 8


def _conv_kernel(lp_ref, lc_ref, ln_ref, gp_ref, gc_ref, gn_ref, cw_ref, cb_ref, lg_ref, lb_ref, o_ref,
                 abuf_ref, ash_ref, ybuf_ref, *, ts, seq):
    first, last, _ = _edge_flags(ts, seq)
    width = lc_ref.shape[1]
    glu = lambda l_ref, g_ref: l_ref[...].astype(F32) * jax.nn.sigmoid(g_ref[...].astype(F32))
    abuf_ref[0:HALO, :] = jnp.where(first, 0.0, glu(lp_ref, gp_ref))
    abuf_ref[HALO + ts:2 * HALO + ts, :] = jnp.where(last, 0.0, glu(ln_ref, gn_ref))
    abuf_ref[HALO:HALO + ts, :] = glu(lc_ref, gc_ref)

    span = ash_ref.shape[1]
    for p in range(1, SUBLANES):
        ash_ref[p - 1] = abuf_ref[p:p + span, :]

    rows = 64 if ts % 64 == 0 else ts
    for r0 in range(0, ts, rows):
        for c0 in range(0, width, 128):
            acc = jnp.zeros((rows, 128), F32)
            for k in range(CONV_WIDTH):
                q, p = divmod(HALO - CONV_PAD + k, SUBLANES)
                start = r0 + q * SUBLANES
                src = abuf_ref if p == 0 else ash_ref.at[p - 1]
                acc = acc + cw_ref[k:k + 1, c0:c0 + 128] * src[start:start + rows, c0:c0 + 128]
            ybuf_ref[r0:r0 + rows, c0:c0 + 128] = acc + cb_ref[:, c0:c0 + 128]

    y = ybuf_ref[...]
    mu = jnp.mean(y, axis=-1, keepdims=True)
    yc = y - mu
    var = jnp.mean(yc * yc, axis=-1, keepdims=True)
    z = yc * lax.rsqrt(var + EPS) * lg_ref[...] + lb_ref[...]
    o_ref[...] = jax.nn.silu(z).astype(o_ref.dtype)


def conv_branch(hin, conv_w, conv_b, ln_g, ln_b, *, seq, bw, ts=256):
    t = hin.shape[0]
    ts = _pick(seq, ts)
    n_tiles = t // ts
    row = lambda v: v.reshape(1, bw)
    const = lambda shape: pl.BlockSpec(shape, lambda i: (0, 0))
    span = ts + 2 * HALO - SUBLANES
    nb = (4 * _nbytes((ts + 2 * HALO, bw), BF16) + 2 * _nbytes((ts, bw), BF16) + _nbytes((ts + 2 * HALO, bw), F32)
          + (SUBLANES - 1) * _nbytes((span, bw), F32) + 5 * _nbytes((ts, bw), F32))
    return pl.pallas_call(
        functools.partial(_conv_kernel, ts=ts, seq=seq),
        grid=(n_tiles,),
        in_specs=_halo_specs(ts, bw, 0, n_tiles) + _halo_specs(ts, bw, 1, n_tiles)
        + [const((CONV_WIDTH, bw)), const((1, bw)), const((1, bw)), const((1, bw))],
        out_specs=pl.BlockSpec((ts, bw), lambda i: (i, 0)),
        out_shape=jax.ShapeDtypeStruct((t, bw), BF16),
        scratch_shapes=[pltpu.VMEM((ts + 2 * HALO, bw), F32), pltpu.VMEM((SUBLANES - 1, span, bw), F32),
                        pltpu.VMEM((ts, bw), F32)],
        compiler_params=_params(("parallel",), nb),
        name="conv_branch",
    )(hin, hin, hin, hin, hin, hin, conv_w, row(conv_b), row(ln_g), row(ln_b))


def _pool_kernel(pp_ref, pc_ref, pn_ref, pw_ref, ps_ref, o_ref, pbuf_ref, *, ts, seq):
    first, last, pos0 = _edge_flags(ts, seq)
    n_g = len(POOL_WINDOWS)
    pg = pc_ref.shape[1] // n_g
    pbuf_ref[0:HALO, :] = jnp.where(first, 0.0, pp_ref[...].astype(F32))
    pbuf_ref[HALO + ts:2 * HALO + ts, :] = jnp.where(last, 0.0, pn_ref[...].astype(F32))
    pbuf_ref[HALO:HALO + ts, :] = pc_ref[...].astype(F32)
    pos = pos0 + lax.broadcasted_iota(jnp.int32, (ts, 1), 0)
    for g, w in enumerate(POOL_WINDOWS):
        cols = slice(g * pg, (g + 1) * pg)
        win = None
        for d in range(-(w // 2), w // 2):
            term = pbuf_ref[HALO + d:HALO + d + ts, cols]
            win = term if win is None else win + term
        lo = jnp.maximum(pos - w // 2, 0)
        hi = jnp.minimum(pos + w // 2 - 1, seq - 1)
        cnt = (hi - lo + 1).astype(F32)
        pooled = win / cnt - pbuf_ref[HALO:HALO + ts, cols]
        mixed = jnp.dot(pooled.astype(BF16), pw_ref[g].astype(BF16), preferred_element_type=F32)
        o_ref[:, cols] = (mixed * ps_ref[:, cols]).astype(o_ref.dtype)


def pool_branch(hin, pool_w, pool_scale, *, seq, bw, col, ts=256):
    t = hin.shape[0]
    ts = _pick(seq, ts)
    n_tiles = t // ts
    nb = 2 * _nbytes((ts + 2 * HALO, bw), F32) + 2 * _nbytes((ts, bw), BF16) + 4 * _nbytes((ts + 2 * HALO, bw), F32)
    nb += 2 * _nbytes(pool_w.shape, F32)
    return pl.pallas_call(
        functools.partial(_pool_kernel, ts=ts, seq=seq),
        grid=(n_tiles,),
        in_specs=_halo_specs(ts, bw, col, n_tiles)
        + [pl.BlockSpec(pool_w.shape, lambda i: (0, 0, 0)), pl.BlockSpec((1, bw), lambda i: (0, 0))],
        out_specs=pl.BlockSpec((ts, bw), lambda i: (i, 0)),
        out_shape=jax.ShapeDtypeStruct((t, bw), BF16),
        scratch_shapes=[pltpu.VMEM((ts + 2 * HALO, bw), F32)],
        compiler_params=_params(("parallel",), nb),
        name="pool_branch",
    )(hin, hin, hin, pool_w, pool_scale.reshape(1, bw))


def _dft_tables(seq, group):
    c = np.arange(group)
    ang = 2.0 * np.pi * np.outer(c, c) / group
    scale = 1.0 / np.sqrt(float(seq) * group)
    chan = np.concatenate([np.cos(ang), np.sin(ang)], axis=1) * scale
    fa = 64 if seq % 64 == 0 else 1
    sp = np.arange(seq)[:, None]
    ang_a = 2.0 * np.pi * ((sp * fa * np.arange(seq // fa)[None, :]) % seq) / seq
    ang_b = 2.0 * np.pi * ((sp * np.arange(fa)[None, :]) % seq) / seq
    ca, sa = jnp.asarray(np.cos(ang_a), F32)[:, :, None], jnp.asarray(np.sin(ang_a), F32)[:, :, None]
    cb, sb = jnp.asarray(np.cos(ang_b), F32)[:, None, :], jnp.asarray(np.sin(ang_b), F32)[:, None, :]
    cos_s = (ca * cb - sa * sb).reshape(seq, seq)
    sin_s = (sa * cb + ca * sb).reshape(seq, seq)
    pos = jnp.concatenate([cos_s, -sin_s], axis=1).astype(BF16)
    return jnp.asarray(chan, F32).astype(BF16), pos


def _chan_dft_kernel(f_ref, tbl_ref, o_ref):
    n_g = N_FFT_GROUPS
    fg = f_ref.shape[1] // n_g
    tbl = tbl_ref[...]
    for g in range(n_g):
        r = jnp.dot(f_ref[:, g * fg:(g + 1) * fg].astype(BF16), tbl, preferred_element_type=F32)
        o_ref[0, :, g * fg:(g + 1) * fg] = r[:, :fg].astype(o_ref.dtype)
        o_ref[1, :, g * fg:(g + 1) * fg] = r[:, fg:].astype(o_ref.dtype)


def fourier_branch(hin, chan_tbl, pos_tbl, *, batch, seq, bw, col, ts=512):
    t = hin.shape[0]
    ts = _pick(seq, ts)
    tiles_per_seq = seq // ts
    nb = 2 * _nbytes((ts, bw), F32) + 4 * _nbytes((ts, bw), BF16) + 2 * _nbytes(chan_tbl.shape, BF16)
    nb += 4 * _nbytes((ts, bw), F32)
    proj = pl.pallas_call(
        _chan_dft_kernel,
        grid=(t // ts,),
        in_specs=[pl.BlockSpec((ts, bw), lambda i: (i, col)), pl.BlockSpec(chan_tbl.shape, lambda i: (0, 0))],
        out_specs=pl.BlockSpec((None, 2, ts, bw), lambda i: (i // tiles_per_seq, 0, i % tiles_per_seq, 0)),
        out_shape=jax.ShapeDtypeStruct((batch, 2, seq, bw), BF16),
        compiler_params=_params(("parallel",), nb),
        name="fourier_channels",
    )(hin, chan_tbl)
    proj = proj.reshape(batch, 2 * seq, bw)
    tm = _pick(seq, 512)
    tn = _pick(bw, 512)
    ni = seq // tm
    nb = 2 * _nbytes((tm, 2 * seq), BF16) + 2 * _nbytes((2 * seq, tn), BF16) + 2 * _nbytes((tm, tn), BF16)
    nb += 2 * _nbytes((tm, tn), F32)
    return pl.pallas_call(
        functools.partial(_matmul_kernel, staged=False, has_res=False, epilogue=None, tk=2 * seq, tn=tn, nj=bw // tn,
                          k0=0),
        grid=(batch, bw // tn, ni),
        in_specs=[pl.BlockSpec((tm, 2 * seq), lambda b, j, i: (i, 0)),
                  pl.BlockSpec((None, 2 * seq, tn), lambda b, j, i: (b, 0, j))],
        out_specs=pl.BlockSpec((tm, tn), lambda b, j, i: (b * ni + i, j)),
        out_shape=jax.ShapeDtypeStruct((t, bw), BF16),
        compiler_params=_params(("parallel", "arbitrary", "arbitrary"), nb),
        name="fourier_positions",
    )(pos_tbl, proj)


def _gmlp_kernel(u_ref, v_ref, lg_ref, lb_ref, ws_ref, bias_ref, o_ref, *, ts):
    v = v_ref[...].astype(F32)
    mu = jnp.mean(v, axis=-1, keepdims=True)
    vc = v - mu
    var = jnp.mean(vc * vc, axis=-1, keepdims=True)
    vn = (vc * lax.rsqrt(var + EPS) * lg_ref[...] + lb_ref[...]).astype(BF16)
    n_chunks = ts // GMLP_CHUNK
    n_heads = v.shape[1] // GMLP_HEAD_DIM
    for h in range(n_heads):
        cols = slice(h * GMLP_HEAD_DIM, (h + 1) * GMLP_HEAD_DIM)
        rhs = jnp.concatenate([vn[c * GMLP_CHUNK:(c + 1) * GMLP_CHUNK, cols] for c in range(n_chunks)], axis=1)
        s = jnp.dot(ws_ref[h].astype(BF16), rhs, preferred_element_type=F32)
        for c in range(n_chunks):
            rows = slice(c * GMLP_CHUNK, (c + 1) * GMLP_CHUNK)
            sc = s[:, c * GMLP_HEAD_DIM:(c + 1) * GMLP_HEAD_DIM] + bias_ref[:, cols]
            o_ref[rows, cols] = (u_ref[rows, cols].astype(F32) * sc).astype(o_ref.dtype)


def gmlp_branch(hin, ln_g, ln_b, ws, bias, *, bw, col_u, col_v, ts=512):
    t = hin.shape[0]
    ts = _pick(t, ts)
    assert ts % GMLP_CHUNK == 0
    n_heads = bw // GMLP_HEAD_DIM
    bias_rows = jnp.repeat(bias.T, GMLP_HEAD_DIM, axis=1)
    nb = 4 * _nbytes((ts, bw), F32) + 2 * _nbytes((ts, bw), BF16) + 6 * _nbytes((ts, bw), F32)
    return pl.pallas_call(
        functools.partial(_gmlp_kernel, ts=ts),
        grid=(t // ts,),
        in_specs=[pl.BlockSpec((ts, bw), lambda i: (i, col_u)), pl.BlockSpec((ts, bw), lambda i: (i, col_v)),
                  pl.BlockSpec((1, bw), lambda i: (0, 0)), pl.BlockSpec((1, bw), lambda i: (0, 0)),
                  pl.BlockSpec((n_heads, GMLP_CHUNK, GMLP_CHUNK), lambda i: (0, 0, 0)),
                  pl.BlockSpec((GMLP_CHUNK, bw), lambda i: (0, 0))],
        out_specs=pl.BlockSpec((ts, bw), lambda i: (i, 0)),
        out_shape=jax.ShapeDtypeStruct((t, bw), BF16),
        compiler_params=_params(("parallel",), nb),
        name="gmlp_branch",
    )(hin, hin, ln_g.reshape(1, bw), ln_b.reshape(1, bw), ws, bias_rows)


ROUTE_BLOCK = 128


def _route(idx, tm):
    a = idx.size
    n_tiles = a // tm + N_EXPERTS
    experts = jnp.arange(N_EXPERTS, dtype=jnp.int32)
    e_flat = idx.reshape(a)
    onehot = (e_flat[:, None] == experts[None, :]).astype(jnp.int32)
    blk = ROUTE_BLOCK if a % ROUTE_BLOCK == 0 else a
    oh_blocks = onehot.astype(F32).reshape(a // blk, blk, N_EXPERTS)
    within = jnp.einsum("ij,bjk->bik", jnp.tril(jnp.ones((blk, blk), F32)), oh_blocks)
    totals = within[:, -1, :]
    offsets = jnp.cumsum(totals, axis=0) - totals
    csum = (within + offsets[:, None, :]).reshape(a, N_EXPERTS).astype(jnp.int32)
    counts = csum[-1]
    rank = jnp.sum((csum - onehot) * onehot, axis=1)
    tiles_e = (counts + tm - 1) // tm
    tile_end = jnp.cumsum(tiles_e)
    tile_start = tile_end - tiles_e
    pos = jnp.sum(onehot * tile_start[None, :], axis=1) * tm + rank
    src_tok = jnp.zeros((n_tiles * tm,), jnp.int32).at[pos].set(jnp.arange(a, dtype=jnp.int32) // TOP_K)
    n_used = tile_end[-1]
    tile_ids = jnp.minimum(jnp.arange(n_tiles, dtype=jnp.int32), n_used - 1)
    tile_expert = jnp.sum((tile_ids[:, None] >= tile_end[None, :]).astype(jnp.int32), axis=1)
    first = jnp.concatenate([jnp.ones((1,), jnp.int32), (tile_expert[1:] != tile_expert[:-1]).astype(jnp.int32)])
    present = tiles_e > 0
    first_present = jnp.min(jnp.where(present, experts, N_EXPERTS))
    last_present = jnp.max(jnp.where(present, experts, -1))
    later = jnp.logical_and(present[None, :], experts[None, :] > experts[:, None])
    next_present = jnp.min(jnp.where(later, experts[None, :], N_EXPERTS), axis=1)
    next_present = jnp.where(next_present == N_EXPERTS, first_present, next_present)
    nxt = next_present[tile_expert]
    last_group = (tile_expert == last_present).astype(jnp.int32)
    rows_valid = counts[tile_expert] - (tile_ids - tile_start[tile_expert]) * tm
    full = (rows_valid > tm // 2).astype(jnp.int32)
    sched = (tile_expert, first, nxt, last_group, full, n_used.reshape(1).astype(jnp.int32))
    return pos, src_tok, sched


def _row_copy(src_hbm, row, dst_ref, r, sem):
    return pltpu.make_async_copy(src_hbm.at[pl.ds(row, 1), :], dst_ref.at[pl.ds(r, 1), :], sem)


ROW_DMA_UNROLL = 8
DMA_PRIORITIES = 2


def _gather_rows_kernel(tok_ref, nrows_ref, src_hbm, o_ref, buf_ref, sem, *, tg, n_steps):
    i = pl.program_id(0)

    def start_step(step):
        slot = step % 2

        def issue(pair, c):
            for prio in range(DMA_PRIORITIES):
                r = pair * DMA_PRIORITIES + prio
                _row_copy(src_hbm, tok_ref[step * tg + r], buf_ref.at[slot], r, sem.at[slot]).start(priority=prio)
            return c

        lax.fori_loop(0, tg // DMA_PRIORITIES, issue, 0, unroll=ROW_DMA_UNROLL // DMA_PRIORITIES)

    def active(step):
        return step * tg < nrows_ref[0]

    @pl.when(i == 0)
    def _():
        start_step(i)

    @pl.when(jnp.logical_and(i + 1 < n_steps, active(i + 1)))
    def _():
        start_step(i + 1)

    @pl.when(active(i))
    def _():
        slot = i % 2

        def drain(r, c):
            _row_copy(src_hbm, 0, buf_ref.at[slot], r, sem.at[slot]).wait()
            return c

        lax.fori_loop(0, tg, drain, 0, unroll=ROW_DMA_UNROLL)
        o_ref[...] = buf_ref[slot].astype(o_ref.dtype)

    @pl.when(jnp.logical_not(active(i)))
    def _():
        o_ref[...] = jnp.zeros_like(o_ref)


def gather_rows(src, tok, n_rows, *, out_dtype, tg=256):
    p = tok.shape[0]
    d = src.shape[1]
    tg = _pick(p, tg)
    n_steps = p // tg
    nb = 2 * _nbytes((tg, d), F32) + 2 * _nbytes((tg, d), out_dtype) + _nbytes((tg, d), F32)
    return pl.pallas_call(
        functools.partial(_gather_rows_kernel, tg=tg, n_steps=n_steps),
        grid_spec=pltpu.PrefetchScalarGridSpec(
            num_scalar_prefetch=2,
            grid=(n_steps,),
            in_specs=[pl.BlockSpec(memory_space=pl.ANY)],
            out_specs=pl.BlockSpec((tg, d), lambda i, tok, nrows: (i, 0)),
            scratch_shapes=[pltpu.VMEM((2, tg, d), F32), pltpu.SemaphoreType.DMA((2,))],
        ),
        out_shape=jax.ShapeDtypeStruct((p, d), out_dtype),
        compiler_params=_params(("arbitrary",), nb),
        name="gather_rows",
    )(tok, n_rows, src)


def _moe_ffn_kernel(te_ref, first_ref, nxt_ref, lastg_ref, full_ref, nu_ref, x_ref, *refs, n_w, tn, nj, tm):
    w_hbm = refs[:n_w]
    o_ref = refs[n_w]
    stage = refs[n_w + 1:2 * n_w + 1]
    wbf = refs[2 * n_w + 1:3 * n_w + 1]
    sem = refs[3 * n_w + 1]
    j = pl.program_id(0)
    n = pl.program_id(1)
    used = n < nu_ref[0]

    def copies(e, jj):
        cols = pl.ds(pl.multiple_of(jj * tn, tn), tn)
        return [pltpu.make_async_copy(w_hbm[q].at[e, :, cols], stage[q], sem.at[q]) for q in range(n_w)]

    @pl.when(jnp.logical_and(used, first_ref[n] == 1))
    def _():
        @pl.when(jnp.logical_and(j == 0, n == 0))
        def _():
            for c in copies(te_ref[n], j):
                c.start()

        for c in copies(te_ref[n], j):
            c.wait()
        _cast_staged(stage, wbf)
        is_last = lastg_ref[n] == 1

        @pl.when(jnp.logical_not(is_last))
        def _():
            for c in copies(nxt_ref[n], j):
                c.start()

        @pl.when(jnp.logical_and(is_last, j + 1 < nj))
        def _():
            for c in copies(nxt_ref[n], j + 1):
                c.start()

    def compute(rows):
        x = x_ref[0:rows, :]
        acc = jnp.dot(x, wbf[0][...], preferred_element_type=F32)
        if n_w == 2:
            acc = jax.nn.silu(acc) * jnp.dot(x, wbf[1][...], preferred_element_type=F32)
        o_ref[0:rows, :] = acc.astype(o_ref.dtype)

    is_full = full_ref[n] == 1

    @pl.when(jnp.logical_and(used, is_full))
    def _():
        compute(tm)

    @pl.when(jnp.logical_and(used, jnp.logical_not(is_full)))
    def _():
        compute(tm // 2)
        o_ref[tm // 2:tm, :] = jnp.zeros((tm - tm // 2, tn), o_ref.dtype)

    @pl.when(jnp.logical_not(used))
    def _():
        o_ref[...] = jnp.zeros_like(o_ref)


def moe_ffn(xs, sched, weights, *, out_dtype, tm, tn=512, name="moe_ffn"):
    p, k = xs.shape
    n_w = len(weights)
    n_out = weights[0].shape[2]
    tn = _pick(n_out, tn)
    nj = n_out // tn
    x_spec = pl.BlockSpec((tm, k), lambda j, n, te, fi, nx, lg, fu, nu: (jnp.minimum(n, nu[0] - 1), 0))
    o_spec = pl.BlockSpec((tm, tn), lambda j, n, te, fi, nx, lg, fu, nu: (n, j))
    nb = (2 * _nbytes((tm, k), BF16) + n_w * _nbytes((k, tn), F32) + n_w * _nbytes((k, tn), BF16)
          + 2 * _nbytes((tm, tn), out_dtype) + (n_w + 1) * _nbytes((tm, tn), F32))
    scratch = ([pltpu.VMEM((k, tn), F32)] * n_w + [pltpu.VMEM((k, tn), BF16)] * n_w
               + [pltpu.SemaphoreType.DMA((n_w,))])
    return pl.pallas_call(
        functools.partial(_moe_ffn_kernel, n_w=n_w, tn=tn, nj=nj, tm=tm),
        grid_spec=pltpu.PrefetchScalarGridSpec(
            num_scalar_prefetch=6,
            grid=(nj, p // tm),
            in_specs=[x_spec] + [pl.BlockSpec(memory_space=pl.ANY)] * n_w,
            out_specs=o_spec,
            scratch_shapes=scratch,
        ),
        out_shape=jax.ShapeDtypeStruct((p, n_out), out_dtype),
        compiler_params=_params(("arbitrary", "arbitrary"), nb),
        name=name,
    )(*sched, xs, *weights)


def _combine_norm_kernel(pos_ref, h_ref, wts_ref, ys_hbm, g_ref, o_ref, buf_ref, sem, *, tg, n_steps):
    i = pl.program_id(0)

    def start_step(step):
        slot = step % 2

        def issue(r, c):
            for k in range(TOP_K):
                row = pos_ref[(step * tg + r) * TOP_K + k]
                _row_copy(ys_hbm, row, buf_ref.at[slot, k], r, sem.at[slot]).start(priority=k % DMA_PRIORITIES)
            return c

        lax.fori_loop(0, tg, issue, 0, unroll=ROW_DMA_UNROLL)

    @pl.when(i == 0)
    def _():
        start_step(i)

    @pl.when(i + 1 < n_steps)
    def _():
        start_step(i + 1)

    slot = i % 2

    def drain(r, c):
        for k in range(TOP_K):
            _row_copy(ys_hbm, 0, buf_ref.at[slot, k], r, sem.at[slot]).wait()
        return c

    lax.fori_loop(0, tg, drain, 0, unroll=ROW_DMA_UNROLL)
    h = h_ref[...]
    for k in range(TOP_K):
        h = h + wts_ref[:, k:k + 1] * buf_ref[slot, k]
    o_ref[...] = _rmsnorm_rows(h, g_ref[...]).astype(o_ref.dtype)


def combine_norm(h, wts, pos, ys, g, *, tg=128):
    t, d = h.shape
    tg = _pick(t, tg)
    n_steps = t // tg
    nb = 4 * _nbytes((tg, d), F32) + 2 * TOP_K * _nbytes((tg, d), F32) + 4 * _nbytes((tg, d), F32)
    return pl.pallas_call(
        functools.partial(_combine_norm_kernel, tg=tg, n_steps=n_steps),
        grid_spec=pltpu.PrefetchScalarGridSpec(
            num_scalar_prefetch=1,
            grid=(n_steps,),
            in_specs=[pl.BlockSpec((tg, d), lambda i, pos: (i, 0)), pl.BlockSpec((tg, TOP_K), lambda i, pos: (i, 0)),
                      pl.BlockSpec(memory_space=pl.ANY), pl.BlockSpec((1, d), lambda i, pos: (0, 0))],
            out_specs=pl.BlockSpec((tg, d), lambda i, pos: (i, 0)),
            scratch_shapes=[pltpu.VMEM((2, TOP_K, tg, d), F32), pltpu.SemaphoreType.DMA((2,))],
        ),
        out_shape=jax.ShapeDtypeStruct((t, d), F32),
        compiler_params=_params(("arbitrary",), nb),
        name="combine_norm",
    )(pos, h, wts, ys, g.reshape(1, d))


def moe_ffn_then_norm(h, norm_g, router, w1, w3, w2, final_g, *, tm=512):
    t = h.shape[0]
    tm = _pick(t * TOP_K, tm)
    hn, idx, wts = rmsnorm_router(h, norm_g, router)
    pos, src_tok, sched = _route(idx, tm)
    xs = gather_rows(hn, src_tok, sched[-1] * tm, out_dtype=BF16)
    acts = moe_ffn(xs, sched, (w1, w3), out_dtype=BF16, tm=tm, name="moe_up")
    ys = moe_ffn(acts, sched, (w2,), out_dtype=F32, tm=tm, tn=1024, name="moe_down")
    return combine_norm(h, wts, pos, ys, final_g)


def _mixer(h, norm1, w_in, w_gate, conv_w, conv_b, conv_ln_g, conv_ln_b, pool_w, pool_scale, gmlp_ln_g, gmlp_ln_b,
           gmlp_ws, gmlp_b, w_branch, w_out, tables, *, batch, seq):
    bw = w_branch.shape[1]
    xn = rmsnorm(h, norm1, BF16)
    hin = matmul(xn, w_in, out_dtype=BF16)
    ya = conv_branch(hin, conv_w, conv_b, conv_ln_g, conv_ln_b, seq=seq, bw=bw)
    yb = pool_branch(hin, pool_w, pool_scale, seq=seq, bw=bw, col=2)
    yc = fourier_branch(hin, *tables, batch=batch, seq=seq, bw=bw, col=3)
    yd = gmlp_branch(hin, gmlp_ln_g, gmlp_ln_b, gmlp_ws, gmlp_b, bw=bw, col_u=4, col_v=5)
    gates = matmul(xn, w_gate, out_dtype=BF16, epilogue="sigmoid")
    merged = merge_branches((ya, yb, yc, yd), gates, w_branch)
    return matmul(merged, w_out, out_dtype=F32, res=h, tm=512)


def kernel(x, l0_norm1, l0_w_in, l0_w_gate, l0_conv_w, l0_conv_b, l0_conv_ln_g, l0_conv_ln_b, l0_pool_w, l0_pool_scale, l0_gmlp_ln_g, l0_gmlp_ln_b, l0_gmlp_ws, l0_gmlp_b, l0_w_branch, l0_w_out, l0_norm2, l0_ffn_w1, l0_ffn_w3, l0_ffn_w2, l1_norm1, l1_w_in, l1_w_gate, l1_conv_w, l1_conv_b, l1_conv_ln_g, l1_conv_ln_b, l1_pool_w, l1_pool_scale, l1_gmlp_ln_g, l1_gmlp_ln_b, l1_gmlp_ws, l1_gmlp_b, l1_w_branch, l1_w_out, l1_norm2, l1_router, l1_exp_w1, l1_exp_w3, l1_exp_w2, final_norm):
    batch, seq, d = x.shape
    bw = l0_w_branch.shape[1]
    tables = _dft_tables(seq, bw // N_FFT_GROUPS)
    h = x.reshape(batch * seq, d)

    h = _mixer(h, l0_norm1, l0_w_in, l0_w_gate, l0_conv_w, l0_conv_b, l0_conv_ln_g, l0_conv_ln_b, l0_pool_w,
               l0_pool_scale, l0_gmlp_ln_g, l0_gmlp_ln_b, l0_gmlp_ws, l0_gmlp_b, l0_w_branch, l0_w_out, tables,
               batch=batch, seq=seq)
    hn = rmsnorm(h, l0_norm2, BF16)
    acts = swiglu_up(hn, l0_ffn_w1, l0_ffn_w3)
    half = l0_ffn_w2.shape[0] // 2
    h = matmul(acts, l0_ffn_w2, out_dtype=F32, res=h, tn=512, k_block=(0, half))
    h = matmul(acts, l0_ffn_w2, out_dtype=F32, res=h, tn=512, k_block=(1, half))

    h = _mixer(h, l1_norm1, l1_w_in, l1_w_gate, l1_conv_w, l1_conv_b, l1_conv_ln_g, l1_conv_ln_b, l1_pool_w,
               l1_pool_scale, l1_gmlp_ln_g, l1_gmlp_ln_b, l1_gmlp_ws, l1_gmlp_b, l1_w_branch, l1_w_out, tables,
               batch=batch, seq=seq)
    out = moe_ffn_then_norm(h, l1_norm2, l1_router, l1_exp_w1, l1_exp_w3, l1_exp_w2, final_norm)
    return out.reshape(batch, seq, d)
```

```python
import functools

import numpy as np
import jax
import jax.numpy as jnp
from jax import lax
from jax.experimental import pallas as pl
from jax.experimental.pallas import tpu as pltpu

F32 = jnp.float32
BF16 = jnp.bfloat16

N_BRANCH = 4
CONV_WIDTH = 31
CONV_PAD = CONV_WIDTH // 2
POOL_WINDOWS = (2, 4, 8, 16)
N_FFT_GROUPS = 4
GMLP_CHUNK = 128
GMLP_HEAD_DIM = 128
N_EXPERTS = 8
TOP_K = 2
EPS = 1e-6

HALO = 16
V7X_VMEM_BYTES = 64 * 1024 * 1024
VMEM_CAP = V7X_VMEM_BYTES - 6 * 1024 * 1024


def _vmem_limit(nbytes):
    return int(min(VMEM_CAP, nbytes + nbytes // 4 + (4 << 20)))


def _nbytes(shape, dtype):
    return int(np.prod(shape)) * jnp.dtype(dtype).itemsize


def _params(sem, nbytes):
    return pltpu.CompilerParams(dimension_semantics=sem, vmem_limit_bytes=_vmem_limit(nbytes))


def _pick(n, pref):
    t = min(n, pref)
    while n % t:
        t -= 8
    return t


def _rmsnorm_rows(x, g):
    return x * lax.rsqrt(jnp.mean(x * x, axis=-1, keepdims=True) + EPS) * g


def _rmsnorm_kernel(x_ref, g_ref, o_ref):
    o_ref[...] = _rmsnorm_rows(x_ref[...], g_ref[...]).astype(o_ref.dtype)


def rmsnorm(x, g, out_dtype):
    t, d = x.shape
    tm = _pick(t, 256)
    nb = 2 * _nbytes((tm, d), F32) + 2 * _nbytes((tm, d), out_dtype)
    return pl.pallas_call(
        _rmsnorm_kernel,
        grid=(t // tm,),
        in_specs=[pl.BlockSpec((tm, d), lambda i: (i, 0)), pl.BlockSpec((1, d), lambda i: (0, 0))],
        out_specs=pl.BlockSpec((tm, d), lambda i: (i, 0)),
        out_shape=jax.ShapeDtypeStruct((t, d), out_dtype),
        compiler_params=_params(("parallel",), nb),
        name="rmsnorm",
    )(x, g.reshape(1, d))


def _rmsnorm_router_kernel(x_ref, g_ref, r_ref, o_ref, idx_ref, wts_ref):
    y = _rmsnorm_rows(x_ref[...], g_ref[...])
    o_ref[...] = y
    n_e = r_ref.shape[1] // 2
    y_hi = y.astype(BF16)
    y_lo = (y - y_hi.astype(F32)).astype(BF16)
    p_hi = jnp.dot(y_hi, r_ref[...], preferred_element_type=F32)
    p_lo = jnp.dot(y_lo, r_ref[...], preferred_element_type=F32)
    logits = p_hi[:, :n_e] + (p_hi[:, n_e:] + p_lo[:, :n_e])
    lane = lax.broadcasted_iota(jnp.int32, logits.shape, 1)
    m1 = jnp.max(logits, axis=-1, keepdims=True)
    i1 = jnp.min(jnp.where(logits == m1, lane, n_e), axis=-1, keepdims=True)
    rest = jnp.where(lane == i1, -jnp.inf, logits)
    m2 = jnp.max(rest, axis=-1, keepdims=True)
    i2 = jnp.min(jnp.where(rest == m2, lane, n_e), axis=-1, keepdims=True)
    e2 = jnp.exp(m2 - m1)
    den = 1.0 + e2
    slot = lax.broadcasted_iota(jnp.int32, idx_ref.shape, 1)
    idx_ref[...] = jnp.where(slot == 0, i1, i2)
    wts_ref[...] = jnp.where(slot == 0, 1.0 / den, e2 / den)


def rmsnorm_router(x, g, router):
    t, d = x.shape
    n_e = router.shape[1]
    tm = _pick(t, 256)
    r_hi = router.astype(BF16)
    r_lo = (router - r_hi.astype(F32)).astype(BF16)
    r_parts = jnp.concatenate([r_hi, r_lo], axis=1)
    nb = 6 * _nbytes((tm, d), F32) + 2 * _nbytes((d, 128), BF16)
    return pl.pallas_call(
        _rmsnorm_router_kernel,
        grid=(t // tm,),
        in_specs=[pl.BlockSpec((tm, d), lambda i: (i, 0)), pl.BlockSpec((1, d), lambda i: (0, 0)),
                  pl.BlockSpec((d, 2 * n_e), lambda i: (0, 0))],
        out_specs=[pl.BlockSpec((tm, d), lambda i: (i, 0)), pl.BlockSpec((tm, TOP_K), lambda i: (i, 0)),
                   pl.BlockSpec((tm, TOP_K), lambda i: (i, 0))],
        out_shape=[jax.ShapeDtypeStruct((t, d), F32), jax.ShapeDtypeStruct((t, TOP_K), jnp.int32),
                   jax.ShapeDtypeStruct((t, TOP_K), F32)],
        compiler_params=_params(("parallel",), nb),
        name="rmsnorm_router",
    )(x, g.reshape(1, d), r_parts)


MXU_COLS = 256
CAST_ROWS = 128


def _col_chunks(tn):
    cn = MXU_COLS if tn % MXU_COLS == 0 else tn
    return [(c0, cn) for c0 in range(0, tn, cn)]


def _cast_staged(stage_refs, wbf_refs):
    k = stage_refs[0].shape[0]

    def body(c, carry):
        rows = pl.ds(pl.multiple_of(c * CAST_ROWS, CAST_ROWS), CAST_ROWS)
        for stage_ref, wbf_ref in zip(stage_refs, wbf_refs):
            wbf_ref[rows, :] = stage_ref[rows, :].astype(BF16)
        return carry

    lax.fori_loop(0, k // CAST_ROWS, body, 0)


def _matmul_kernel(*refs, has_res, epilogue, tk, tn, nj, k0):
    x_ref, w_ref = refs[0], refs[1]
    pos = 2
    r_ref = None
    if has_res:
        r_ref = refs[pos]
        pos += 1
    o_ref = refs[pos]
    stage_ref, wbf_ref, sem = refs[pos + 1:pos + 4]
    j = pl.program_id(0)

    def copy(jj):
        cols = pl.ds(pl.multiple_of(jj * tn, tn), tn)
        return pltpu.make_async_copy(w_ref.at[pl.ds(k0, tk), cols], stage_ref, sem)

    @pl.when(pl.program_id(1) == 0)
    def _():
        @pl.when(j == 0)
        def _():
            copy(j).start()

        copy(j).wait()
        _cast_staged([stage_ref], [wbf_ref])

        @pl.when(j + 1 < nj)
        def _():
            copy(j + 1).start()

    acc = jnp.dot(x_ref[...], wbf_ref[...], preferred_element_type=F32)
    if epilogue == "sigmoid":
        acc = jax.nn.sigmoid(acc)
    if has_res:
        acc = acc + r_ref[...]
    o_ref[...] = acc.astype(o_ref.dtype)


def matmul(x, w, *, out_dtype, res=None, epilogue=None, tm=1024, tn=1024, k_block=None):
    m, kx = x.shape
    kw, n = w.shape
    assert w.dtype == F32
    if k_block is None:
        assert kx == kw
        kk, tk = 0, kx
    else:
        kk, tk = k_block
    tm = _pick(m, tm)
    tn = _pick(n, tn)
    nj = n // tn
    nb = 2 * _nbytes((tm, tk), BF16) + 2 * _nbytes((tm, tn), out_dtype) + 2 * _nbytes((tm, tn), F32)
    scratch = [pltpu.VMEM((tk, tn), F32), pltpu.VMEM((tk, tn), BF16), pltpu.SemaphoreType.DMA(())]
    nb += _nbytes((tk, tn), F32) + _nbytes((tk, tn), BF16)
    in_specs = [pl.BlockSpec((tm, tk), lambda j, i: (i, kk)), pl.BlockSpec(memory_space=pl.ANY)]
    args = [x, w]
    if res is not None:
        in_specs.append(pl.BlockSpec((tm, tn), lambda j, i: (i, j)))
        args.append(res)
        nb += 2 * _nbytes((tm, tn), res.dtype)
    return pl.pallas_call(
        functools.partial(_matmul_kernel, has_res=res is not None, epilogue=epilogue,
                          tk=tk, tn=tn, nj=nj, k0=kk * tk),
        grid=(nj, m // tm),
        in_specs=in_specs,
        out_specs=pl.BlockSpec((tm, tn), lambda j, i: (i, j)),
        out_shape=jax.ShapeDtypeStruct((m, n), out_dtype),
        scratch_shapes=scratch,
        compiler_params=_params(("arbitrary", "arbitrary"), nb),
        name="matmul",
    )(*args)


def swiglu_up(x, w1, w3, *, tm=2048, tn=256):
    tm = _pick(x.shape[0], tm)
    n_tiles = x.shape[0] // tm
    zeros = jnp.zeros((n_tiles,), jnp.int32)
    ones = jnp.ones((n_tiles,), jnp.int32)
    first = zeros.at[0].set(1)
    sched = (zeros, first, zeros, ones, ones, jnp.full((1,), n_tiles, jnp.int32))
    return moe_ffn(x, sched, (w1[None], w3[None]), out_dtype=BF16, tm=tm, tn=tn, name="swiglu_up")


def _merge_kernel(ya_ref, yb_ref, yc_ref, yd_ref, g0_ref, g1_ref, g2_ref, g3_ref, wb_ref, o_ref, wbf_ref):
    @pl.when(pl.program_id(1) == 0)
    def _():
        wbf_ref[...] = wb_ref[...].astype(BF16)

    pairs = ((ya_ref, g0_ref), (yb_ref, g1_ref), (yc_ref, g2_ref), (yd_ref, g3_ref))
    for c0, cn in _col_chunks(o_ref.shape[1]):
        acc = None
        for g, (y_ref, g_ref) in enumerate(pairs):
            proj = jnp.dot(y_ref[...], wbf_ref[g, :, c0:c0 + cn], preferred_element_type=F32)
            term = g_ref[:, c0:c0 + cn].astype(F32) * proj
            acc = term if acc is None else acc + term
        o_ref[:, c0:c0 + cn] = acc.astype(o_ref.dtype)


def merge_branches(branches, gates, w_branch, *, tm=1024, tn=512):
    t, bw = branches[0].shape
    d = w_branch.shape[2]
    tm = _pick(t, tm)
    tn = _pick(d, tn)
    nj = d // tn
    y_spec = pl.BlockSpec((tm, bw), lambda j, i: (i, 0))
    gate_specs = [pl.BlockSpec((tm, tn), functools.partial(lambda j, i, g: (i, g * nj + j), g=g))
                  for g in range(N_BRANCH)]
    nb = (2 * N_BRANCH * _nbytes((tm, bw), BF16) + 2 * N_BRANCH * _nbytes((tm, tn), BF16)
          + 2 * _nbytes((N_BRANCH, bw, tn), F32) + _nbytes((N_BRANCH, bw, tn), BF16)
          + 2 * _nbytes((tm, tn), BF16) + 3 * _nbytes((tm, tn), F32))
    return pl.pallas_call(
        _merge_kernel,
        grid=(nj, t // tm),
        in_specs=[y_spec] * N_BRANCH + gate_specs + [pl.BlockSpec((N_BRANCH, bw, tn), lambda j, i: (0, 0, j))],
        out_specs=pl.BlockSpec((tm, tn), lambda j, i: (i, j)),
        out_shape=jax.ShapeDtypeStruct((t, d), BF16),
        scratch_shapes=[pltpu.VMEM((N_BRANCH, bw, tn), BF16)],
        compiler_params=_params(("arbitrary", "arbitrary"), nb),
        name="merge_branches",
    )(*branches, gates, gates, gates, gates, w_branch)


def _halo_specs(ts, width, col, seq_tiles_total):
    r = ts // HALO
    last = seq_tiles_total * r - 1
    prev = pl.BlockSpec((HALO, width), lambda i: (jnp.maximum(i * r - 1, 0), col))
    cur = pl.BlockSpec((ts, width), lambda i: (i, col))
    nxt = pl.BlockSpec((HALO, width), lambda i: (jnp.minimum((i + 1) * r, last), col))
    return [prev, cur, nxt]


def _edge_flags(ts, seq):
    tiles_per_seq = seq // ts
    k = pl.program_id(0) % tiles_per_seq
    return k == 0, k == tiles_per_seq - 1, k * ts


SUBLANES = 8


def _conv_kernel(lp_ref, lc_ref, ln_ref, gp_ref, gc_ref, gn_ref, cw_ref, cb_ref, lg_ref, lb_ref, o_ref,
                 abuf_ref, ash_ref, ybuf_ref, *, ts, seq):
    first, last, _ = _edge_flags(ts, seq)
    width = lc_ref.shape[1]
    glu = lambda l_ref, g_ref: l_ref[...].astype(F32) * jax.nn.sigmoid(g_ref[...].astype(F32))
    abuf_ref[0:HALO, :] = jnp.where(first, 0.0, glu(lp_ref, gp_ref))
    abuf_ref[HALO + ts:2 * HALO + ts, :] = jnp.where(last, 0.0, glu(ln_ref, gn_ref))
    abuf_ref[HALO:HALO + ts, :] = glu(lc_ref, gc_ref)

    span = ash_ref.shape[1]
    for p in range(1, SUBLANES):
        ash_ref[p - 1] = abuf_ref[p:p + span, :]

    rows = 64 if ts % 64 == 0 else ts
    for r0 in range(0, ts, rows):
        for c0 in range(0, width, 128):
            acc = jnp.zeros((rows, 128), F32)
            for k in range(CONV_WIDTH):
                q, p = divmod(HALO - CONV_PAD + k, SUBLANES)
                start = r0 + q * SUBLANES
                src = abuf_ref if p == 0 else ash_ref.at[p - 1]
                acc = acc + cw_ref[k:k + 1, c0:c0 + 128] * src[start:start + rows, c0:c0 + 128]
            ybuf_ref[r0:r0 + rows, c0:c0 + 128] = acc + cb_ref[:, c0:c0 + 128]

    y = ybuf_ref[...]
    mu = jnp.mean(y, axis=-1, keepdims=True)
    yc = y - mu
    var = jnp.mean(yc * yc, axis=-1, keepdims=True)
    z = yc * lax.rsqrt(var + EPS) * lg_ref[...] + lb_ref[...]
    o_ref[...] = jax.nn.silu(z).astype(o_ref.dtype)


def conv_branch(hin, conv_w, conv_b, ln_g, ln_b, *, seq, bw, ts=256):
    t = hin.shape[0]
    ts = _pick(seq, ts)
    n_tiles = t // ts
    row = lambda v: v.reshape(1, bw)
    const = lambda shape: pl.BlockSpec(shape, lambda i: (0, 0))
    span = ts + 2 * HALO - SUBLANES
    nb = (4 * _nbytes((ts + 2 * HALO, bw), BF16) + 2 * _nbytes((ts, bw), BF16) + _nbytes((ts + 2 * HALO, bw), F32)
          + (SUBLANES - 1) * _nbytes((span, bw), F32) + 5 * _nbytes((ts, bw), F32))
    return pl.pallas_call(
        functools.partial(_conv_kernel, ts=ts, seq=seq),
        grid=(n_tiles,),
        in_specs=_halo_specs(ts, bw, 0, n_tiles) + _halo_specs(ts, bw, 1, n_tiles)
        + [const((CONV_WIDTH, bw)), const((1, bw)), const((1, bw)), const((1, bw))],
        out_specs=pl.BlockSpec((ts, bw), lambda i: (i, 0)),
        out_shape=jax.ShapeDtypeStruct((t, bw), BF16),
        scratch_shapes=[pltpu.VMEM((ts + 2 * HALO, bw), F32), pltpu.VMEM((SUBLANES - 1, span, bw), F32),
                        pltpu.VMEM((ts, bw), F32)],
        compiler_params=_params(("parallel",), nb),
        name="conv_branch",
    )(hin, hin, hin, hin, hin, hin, conv_w, row(conv_b), row(ln_g), row(ln_b))


def _pool_kernel(pp_ref, pc_ref, pn_ref, pw_ref, ps_ref, o_ref, pbuf_ref, *, ts, seq):
    first, last, pos0 = _edge_flags(ts, seq)
    n_g = len(POOL_WINDOWS)
    pg = pc_ref.shape[1] // n_g
    pbuf_ref[0:HALO, :] = jnp.where(first, 0.0, pp_ref[...].astype(F32))
    pbuf_ref[HALO + ts:2 * HALO + ts, :] = jnp.where(last, 0.0, pn_ref[...].astype(F32))
    pbuf_ref[HALO:HALO + ts, :] = pc_ref[...].astype(F32)
    pos = pos0 + lax.broadcasted_iota(jnp.int32, (ts, 1), 0)
    for g, w in enumerate(POOL_WINDOWS):
        cols = slice(g * pg, (g + 1) * pg)
        win = None
        for d in range(-(w // 2), w // 2):
            term = pbuf_ref[HALO + d:HALO + d + ts, cols]
            win = term if win is None else win + term
        lo = jnp.maximum(pos - w // 2, 0)
        hi = jnp.minimum(pos + w // 2 - 1, seq - 1)
        cnt = (hi - lo + 1).astype(F32)
        pooled = win / cnt - pbuf_ref[HALO:HALO + ts, cols]
        mixed = jnp.dot(pooled.astype(BF16), pw_ref[g].astype(BF16), preferred_element_type=F32)
        o_ref[:, cols] = (mixed * ps_ref[:, cols]).astype(o_ref.dtype)


def pool_branch(hin, pool_w, pool_scale, *, seq, bw, col, ts=256):
    t = hin.shape[0]
    ts = _pick(seq, ts)
    n_tiles = t // ts
    nb = 2 * _nbytes((ts + 2 * HALO, bw), F32) + 2 * _nbytes((ts, bw), BF16) + 4 * _nbytes((ts + 2 * HALO, bw), F32)
    nb += 2 * _nbytes(pool_w.shape, F32)
    return pl.pallas_call(
        functools.partial(_pool_kernel, ts=ts, seq=seq),
        grid=(n_tiles,),
        in_specs=_halo_specs(ts, bw, col, n_tiles)
        + [pl.BlockSpec(pool_w.shape, lambda i: (0, 0, 0)), pl.BlockSpec((1, bw), lambda i: (0, 0))],
        out_specs=pl.BlockSpec((ts, bw), lambda i: (i, 0)),
        out_shape=jax.ShapeDtypeStruct((t, bw), BF16),
        scratch_shapes=[pltpu.VMEM((ts + 2 * HALO, bw), F32)],
        compiler_params=_params(("parallel",), nb),
        name="pool_branch",
    )(hin, hin, hin, pool_w, pool_scale.reshape(1, bw))


def _dft_tables(seq, group):
    c = np.arange(group)
    ang = 2.0 * np.pi * np.outer(c, c) / group
    scale = 1.0 / np.sqrt(float(seq) * group)
    chan = np.concatenate([np.cos(ang), np.sin(ang)], axis=1) * scale
    half = seq // 2
    fa = 64 if half % 64 == 0 else 1
    sp = np.arange(seq)[:, None]
    ang_a = 2.0 * np.pi * ((sp * fa * np.arange(half // fa)[None, :]) % seq) / seq
    ang_b = 2.0 * np.pi * ((sp * np.arange(fa)[None, :]) % seq) / seq
    ca, sa = jnp.asarray(np.cos(ang_a), F32)[:, :, None], jnp.asarray(np.sin(ang_a), F32)[:, :, None]
    cb, sb = jnp.asarray(np.cos(ang_b), F32)[:, None, :], jnp.asarray(np.sin(ang_b), F32)[:, None, :]
    cos_s = (ca * cb - sa * sb).reshape(seq, half)
    sin_s = (sa * cb + ca * sb).reshape(seq, half)
    pos = jnp.concatenate([cos_s, -sin_s], axis=1).astype(BF16)
    return jnp.asarray(chan, F32).astype(BF16), pos


FOLD_EDGE_ROWS = 16


def _fold_chan_dft_kernel(fc_ref, fm_ref, fx_ref, flip_ref, tbl_ref, o_ref, eh_ref, *, tiles_half):
    i = pl.program_id(0) % tiles_half
    n_g = N_FFT_GROUPS
    fg = fc_ref.shape[1] // n_g
    ts = fc_ref.shape[0]
    tbl = tbl_ref[...]
    mirror = jnp.dot(flip_ref[...], fm_ref[...], preferred_element_type=F32)
    row = lax.broadcasted_iota(jnp.int32, (ts, 1), 0)
    edge = fx_ref[0:1, :].astype(F32)
    partner = jnp.where(jnp.logical_and(row == 0, i > 0), edge, mirror)
    f = fc_ref[...].astype(F32)
    even = (f + partner).astype(BF16)
    odd = (f - partner).astype(BF16)
    for g in range(n_g):
        cols = slice(g * fg, (g + 1) * fg)
        o_ref[0, :, cols] = jnp.dot(even[:, cols], tbl[:, :fg], preferred_element_type=F32).astype(o_ref.dtype)
        o_ref[1, :, cols] = jnp.dot(odd[:, cols], tbl[:, fg:], preferred_element_type=F32).astype(o_ref.dtype)

    @pl.when(i == 0)
    def _():
        edge_row = lax.broadcasted_iota(jnp.int32, (FOLD_EDGE_ROWS, 1), 0)
        mid = jnp.where(edge_row == 0, fx_ref[...].astype(F32), 0.0).astype(BF16)
        for g in range(n_g):
            cols = slice(g * fg, (g + 1) * fg)
            eh_ref[:, cols] = jnp.dot(mid[:, cols], tbl[:, :fg], preferred_element_type=F32).astype(eh_ref.dtype)


def _fourier_pos_kernel(t_ref, r_ref, eh_ref, o_ref):
    acc = jnp.dot(t_ref[...], r_ref[...], preferred_element_type=F32)
    row = lax.broadcasted_iota(jnp.int32, (t_ref.shape[0], 1), 0)
    sign = jnp.where(jnp.bitwise_and(row, 1) == 0, 1.0, -1.0)
    o_ref[...] = (acc + sign * eh_ref[0:1, :].astype(F32)).astype(o_ref.dtype)


def fourier_branch(hin, chan_tbl, pos_tbl, *, batch, seq, bw, col, ts=512):
    t = hin.shape[0]
    half = seq // 2
    ts = _pick(half, ts)
    assert ts % FOLD_EDGE_ROWS == 0
    tiles = seq // ts
    th = tiles // 2
    edge_blocks = seq // FOLD_EDGE_ROWS
    flip = np.zeros((ts, ts), np.float32)
    flip[np.arange(1, ts), ts - np.arange(1, ts)] = 1.0
    flip = jnp.asarray(flip, BF16)

    def edge_block(s):
        i = s % th
        pos = jnp.where(i == 0, half, seq - i * ts)
        return (s // th) * edge_blocks + pos // FOLD_EDGE_ROWS, col

    nb = 6 * _nbytes((ts, bw), BF16) + 2 * _nbytes(chan_tbl.shape, BF16) + 2 * _nbytes((ts, ts), BF16)
    nb += 6 * _nbytes((ts, bw), F32)
    proj, mid = pl.pallas_call(
        functools.partial(_fold_chan_dft_kernel, tiles_half=th),
        grid=(batch * th,),
        in_specs=[pl.BlockSpec((ts, bw), lambda s: ((s // th) * tiles + s % th, col)),
                  pl.BlockSpec((ts, bw), lambda s: ((s // th) * tiles + tiles - 1 - s % th, col)),
                  pl.BlockSpec((FOLD_EDGE_ROWS, bw), edge_block),
                  pl.BlockSpec((ts, ts), lambda s: (0, 0)), pl.BlockSpec(chan_tbl.shape, lambda s: (0, 0))],
        out_specs=[pl.BlockSpec((None, 2, ts, bw), lambda s: (s // th, 0, s % th, 0)),
                   pl.BlockSpec((None, FOLD_EDGE_ROWS, bw), lambda s: (s // th, 0, 0))],
        out_shape=[jax.ShapeDtypeStruct((batch, 2, half, bw), BF16),
                   jax.ShapeDtypeStruct((batch, FOLD_EDGE_ROWS, bw), BF16)],
        compiler_params=_params(("arbitrary",), nb),
        name="fourier_channels",
    )(hin, hin, hin, flip, chan_tbl)
    proj = proj.reshape(batch, seq, bw)
    tm = _pick(seq, 1024)
    tn = _pick(bw, 512)
    ni = seq // tm
    nb = 2 * _nbytes((tm, seq), BF16) + 2 * _nbytes((seq, tn), BF16) + 2 * _nbytes((tm, tn), BF16)
    nb += 2 * _nbytes((tm, tn), F32)
    return pl.pallas_call(
        _fourier_pos_kernel,
        grid=(batch, bw // tn, ni),
        in_specs=[pl.BlockSpec((tm, seq), lambda b, j, i: (i, 0)),
                  pl.BlockSpec((None, seq, tn), lambda b, j, i: (b, 0, j)),
                  pl.BlockSpec((None, FOLD_EDGE_ROWS, tn), lambda b, j, i: (b, 0, j))],
        out_specs=pl.BlockSpec((tm, tn), lambda b, j, i: (b * ni + i, j)),
        out_shape=jax.ShapeDtypeStruct((t, bw), BF16),
        compiler_params=_params(("parallel", "arbitrary", "arbitrary"), nb),
        name="fourier_positions",
    )(pos_tbl, proj, mid)


def _gmlp_kernel(u_ref, v_ref, lg_ref, lb_ref, ws_ref, bias_ref, o_ref, *, ts):
    v = v_ref[...].astype(F32)
    mu = jnp.mean(v, axis=-1, keepdims=True)
    vc = v - mu
    var = jnp.mean(vc * vc, axis=-1, keepdims=True)
    vn = (vc * lax.rsqrt(var + EPS) * lg_ref[...] + lb_ref[...]).astype(BF16)
    n_chunks = ts // GMLP_CHUNK
    n_heads = v.shape[1] // GMLP_HEAD_DIM
    for h in range(n_heads):
        cols = slice(h * GMLP_HEAD_DIM, (h + 1) * GMLP_HEAD_DIM)
        rhs = jnp.concatenate([vn[c * GMLP_CHUNK:(c + 1) * GMLP_CHUNK, cols] for c in range(n_chunks)], axis=1)
        s = jnp.dot(ws_ref[h].astype(BF16), rhs, preferred_element_type=F32)
        for c in range(n_chunks):
            rows = slice(c * GMLP_CHUNK, (c + 1) * GMLP_CHUNK)
            sc = s[:, c * GMLP_HEAD_DIM:(c + 1) * GMLP_HEAD_DIM] + bias_ref[:, cols]
            o_ref[rows, cols] = (u_ref[rows, cols].astype(F32) * sc).astype(o_ref.dtype)


def gmlp_branch(hin, ln_g, ln_b, ws, bias, *, bw, col_u, col_v, ts=512):
    t = hin.shape[0]
    ts = _pick(t, ts)
    assert ts % GMLP_CHUNK == 0
    n_heads = bw // GMLP_HEAD_DIM
    bias_rows = jnp.repeat(bias.T, GMLP_HEAD_DIM, axis=1)
    nb = 4 * _nbytes((ts, bw), F32) + 2 * _nbytes((ts, bw), BF16) + 6 * _nbytes((ts, bw), F32)
    return pl.pallas_call(
        functools.partial(_gmlp_kernel, ts=ts),
        grid=(t // ts,),
        in_specs=[pl.BlockSpec((ts, bw), lambda i: (i, col_u)), pl.BlockSpec((ts, bw), lambda i: (i, col_v)),
                  pl.BlockSpec((1, bw), lambda i: (0, 0)), pl.BlockSpec((1, bw), lambda i: (0, 0)),
                  pl.BlockSpec((n_heads, GMLP_CHUNK, GMLP_CHUNK), lambda i: (0, 0, 0)),
                  pl.BlockSpec((GMLP_CHUNK, bw), lambda i: (0, 0))],
        out_specs=pl.BlockSpec((ts, bw), lambda i: (i, 0)),
        out_shape=jax.ShapeDtypeStruct((t, bw), BF16),
        compiler_params=_params(("parallel",), nb),
        name="gmlp_branch",
    )(hin, hin, ln_g.reshape(1, bw), ln_b.reshape(1, bw), ws, bias_rows)


ROUTE_BLOCK = 128


def _route(idx, tm):
    a = idx.size
    n_tiles = a // tm + N_EXPERTS
    experts = jnp.arange(N_EXPERTS, dtype=jnp.int32)
    e_flat = idx.reshape(a)
    onehot = (e_flat[:, None] == experts[None, :]).astype(jnp.int32)
    blk = ROUTE_BLOCK if a % ROUTE_BLOCK == 0 else a
    oh_blocks = onehot.astype(F32).reshape(a // blk, blk, N_EXPERTS)
    within = jnp.einsum("ij,bjk->bik", jnp.tril(jnp.ones((blk, blk), F32)), oh_blocks)
    totals = within[:, -1, :]
    offsets = jnp.cumsum(totals, axis=0) - totals
    csum = (within + offsets[:, None, :]).reshape(a, N_EXPERTS).astype(jnp.int32)
    counts = csum[-1]
    rank = jnp.sum((csum - onehot) * onehot, axis=1)
    tiles_e = (counts + tm - 1) // tm
    tile_end = jnp.cumsum(tiles_e)
    tile_start = tile_end - tiles_e
    pos = jnp.sum(onehot * tile_start[None, :], axis=1) * tm + rank
    src_tok = jnp.zeros((n_tiles * tm,), jnp.int32).at[pos].set(jnp.arange(a, dtype=jnp.int32) // TOP_K)
    n_used = tile_end[-1]
    tile_ids = jnp.minimum(jnp.arange(n_tiles, dtype=jnp.int32), n_used - 1)
    tile_expert = jnp.sum((tile_ids[:, None] >= tile_end[None, :]).astype(jnp.int32), axis=1)
    first = jnp.concatenate([jnp.ones((1,), jnp.int32), (tile_expert[1:] != tile_expert[:-1]).astype(jnp.int32)])
    present = tiles_e > 0
    first_present = jnp.min(jnp.where(present, experts, N_EXPERTS))
    last_present = jnp.max(jnp.where(present, experts, -1))
    later = jnp.logical_and(present[None, :], experts[None, :] > experts[:, None])
    next_present = jnp.min(jnp.where(later, experts[None, :], N_EXPERTS), axis=1)
    next_present = jnp.where(next_present == N_EXPERTS, first_present, next_present)
    nxt = next_present[tile_expert]
    last_group = (tile_expert == last_present).astype(jnp.int32)
    rows_valid = counts[tile_expert] - (tile_ids - tile_start[tile_expert]) * tm
    full = (rows_valid > tm // 2).astype(jnp.int32)
    sched = (tile_expert, first, nxt, last_group, full, n_used.reshape(1).astype(jnp.int32))
    return pos, src_tok, sched


def _row_copy(src_hbm, row, dst_ref, r, sem):
    return pltpu.make_async_copy(src_hbm.at[pl.ds(row, 1), :], dst_ref.at[pl.ds(r, 1), :], sem)


ROW_DMA_UNROLL = 8
DMA_PRIORITIES = 2


def _gather_rows_kernel(tok_ref, nrows_ref, src_hbm, o_ref, buf_ref, sem, *, tg, n_steps):
    i = pl.program_id(0)

    def start_step(step):
        slot = step % 2

        def issue(pair, c):
            for prio in range(DMA_PRIORITIES):
                r = pair * DMA_PRIORITIES + prio
                _row_copy(src_hbm, tok_ref[step * tg + r], buf_ref.at[slot], r, sem.at[slot]).start(priority=prio)
            return c

        lax.fori_loop(0, tg // DMA_PRIORITIES, issue, 0, unroll=ROW_DMA_UNROLL // DMA_PRIORITIES)

    def active(step):
        return step * tg < nrows_ref[0]

    @pl.when(i == 0)
    def _():
        start_step(i)

    @pl.when(jnp.logical_and(i + 1 < n_steps, active(i + 1)))
    def _():
        start_step(i + 1)

    @pl.when(active(i))
    def _():
        slot = i % 2

        def drain(r, c):
            _row_copy(src_hbm, 0, buf_ref.at[slot], r, sem.at[slot]).wait()
            return c

        lax.fori_loop(0, tg, drain, 0, unroll=ROW_DMA_UNROLL)
        o_ref[...] = buf_ref[slot].astype(o_ref.dtype)

    @pl.when(jnp.logical_not(active(i)))
    def _():
        o_ref[...] = jnp.zeros_like(o_ref)


def gather_rows(src, tok, n_rows, *, out_dtype, tg=256):
    p = tok.shape[0]
    d = src.shape[1]
    tg = _pick(p, tg)
    n_steps = p // tg
    nb = 2 * _nbytes((tg, d), F32) + 2 * _nbytes((tg, d), out_dtype) + _nbytes((tg, d), F32)
    return pl.pallas_call(
        functools.partial(_gather_rows_kernel, tg=tg, n_steps=n_steps),
        grid_spec=pltpu.PrefetchScalarGridSpec(
            num_scalar_prefetch=2,
            grid=(n_steps,),
            in_specs=[pl.BlockSpec(memory_space=pl.ANY)],
            out_specs=pl.BlockSpec((tg, d), lambda i, tok, nrows: (i, 0)),
            scratch_shapes=[pltpu.VMEM((2, tg, d), F32), pltpu.SemaphoreType.DMA((2,))],
        ),
        out_shape=jax.ShapeDtypeStruct((p, d), out_dtype),
        compiler_params=_params(("arbitrary",), nb),
        name="gather_rows",
    )(tok, n_rows, src)


def _moe_ffn_kernel(te_ref, first_ref, nxt_ref, lastg_ref, full_ref, nu_ref, x_ref, *refs, n_w, tn, nj, tm):
    w_hbm = refs[:n_w]
    o_ref = refs[n_w]
    stage = refs[n_w + 1:2 * n_w + 1]
    wbf = refs[2 * n_w + 1:3 * n_w + 1]
    sem = refs[3 * n_w + 1]
    j = pl.program_id(0)
    n = pl.program_id(1)
    used = n < nu_ref[0]

    def copies(e, jj):
        cols = pl.ds(pl.multiple_of(jj * tn, tn), tn)
        return [pltpu.make_async_copy(w_hbm[q].at[e, :, cols], stage[q], sem.at[q]) for q in range(n_w)]

    @pl.when(jnp.logical_and(used, first_ref[n] == 1))
    def _():
        @pl.when(jnp.logical_and(j == 0, n == 0))
        def _():
            for c in copies(te_ref[n], j):
                c.start()

        for c in copies(te_ref[n], j):
            c.wait()
        _cast_staged(stage, wbf)
        is_last = lastg_ref[n] == 1

        @pl.when(jnp.logical_not(is_last))
        def _():
            for c in copies(nxt_ref[n], j):
                c.start()

        @pl.when(jnp.logical_and(is_last, j + 1 < nj))
        def _():
            for c in copies(nxt_ref[n], j + 1):
                c.start()

    def compute(rows):
        x = x_ref[0:rows, :]
        acc = jnp.dot(x, wbf[0][...], preferred_element_type=F32)
        if n_w == 2:
            acc = jax.nn.silu(acc) * jnp.dot(x, wbf[1][...], preferred_element_type=F32)
        o_ref[0:rows, :] = acc.astype(o_ref.dtype)

    is_full = full_ref[n] == 1

    @pl.when(jnp.logical_and(used, is_full))
    def _():
        compute(tm)

    @pl.when(jnp.logical_and(used, jnp.logical_not(is_full)))
    def _():
        compute(tm // 2)
        o_ref[tm // 2:tm, :] = jnp.zeros((tm - tm // 2, tn), o_ref.dtype)

    @pl.when(jnp.logical_not(used))
    def _():
        o_ref[...] = jnp.zeros_like(o_ref)


def moe_ffn(xs, sched, weights, *, out_dtype, tm, tn=512, name="moe_ffn"):
    p, k = xs.shape
    n_w = len(weights)
    n_out = weights[0].shape[2]
    tn = _pick(n_out, tn)
    nj = n_out // tn
    x_spec = pl.BlockSpec((tm, k), lambda j, n, te, fi, nx, lg, fu, nu: (jnp.minimum(n, nu[0] - 1), 0))
    o_spec = pl.BlockSpec((tm, tn), lambda j, n, te, fi, nx, lg, fu, nu: (n, j))
    nb = (2 * _nbytes((tm, k), BF16) + n_w * _nbytes((k, tn), F32) + n_w * _nbytes((k, tn), BF16)
          + 2 * _nbytes((tm, tn), out_dtype) + (n_w + 1) * _nbytes((tm, tn), F32))
    scratch = ([pltpu.VMEM((k, tn), F32)] * n_w + [pltpu.VMEM((k, tn), BF16)] * n_w
               + [pltpu.SemaphoreType.DMA((n_w,))])
    return pl.pallas_call(
        functools.partial(_moe_ffn_kernel, n_w=n_w, tn=tn, nj=nj, tm=tm),
        grid_spec=pltpu.PrefetchScalarGridSpec(
            num_scalar_prefetch=6,
            grid=(nj, p // tm),
            in_specs=[x_spec] + [pl.BlockSpec(memory_space=pl.ANY)] * n_w,
            out_specs=o_spec,
            scratch_shapes=scratch,
        ),
        out_shape=jax.ShapeDtypeStruct((p, n_out), out_dtype),
        compiler_params=_params(("arbitrary", "arbitrary"), nb),
        name=name,
    )(*sched, xs, *weights)


def _combine_norm_kernel(pos_ref, h_ref, wts_ref, ys_hbm, g_ref, o_ref, buf_ref, sem, *, tg, n_steps):
    i = pl.program_id(0)

    def start_step(step):
        slot = step % 2

        def issue(r, c):
            for k in range(TOP_K):
                row = pos_ref[(step * tg + r) * TOP_K + k]
                _row_copy(ys_hbm, row, buf_ref.at[slot, k], r, sem.at[slot]).start(priority=k % DMA_PRIORITIES)
            return c

        lax.fori_loop(0, tg, issue, 0, unroll=ROW_DMA_UNROLL)

    @pl.when(i == 0)
    def _():
        start_step(i)

    @pl.when(i + 1 < n_steps)
    def _():
        start_step(i + 1)

    slot = i % 2

    def drain(r, c):
        for k in range(TOP_K):
            _row_copy(ys_hbm, 0, buf_ref.at[slot, k], r, sem.at[slot]).wait()
        return c

    lax.fori_loop(0, tg, drain, 0, unroll=ROW_DMA_UNROLL)
    h = h_ref[...]
    for k in range(TOP_K):
        h = h + wts_ref[:, k:k + 1] * buf_ref[slot, k]
    o_ref[...] = _rmsnorm_rows(h, g_ref[...]).astype(o_ref.dtype)


def combine_norm(h, wts, pos, ys, g, *, tg=128):
    t, d = h.shape
    tg = _pick(t, tg)
    n_steps = t // tg
    nb = 4 * _nbytes((tg, d), F32) + 2 * TOP_K * _nbytes((tg, d), F32) + 4 * _nbytes((tg, d), F32)
    return pl.pallas_call(
        functools.partial(_combine_norm_kernel, tg=tg, n_steps=n_steps),
        grid_spec=pltpu.PrefetchScalarGridSpec(
            num_scalar_prefetch=1,
            grid=(n_steps,),
            in_specs=[pl.BlockSpec((tg, d), lambda i, pos: (i, 0)), pl.BlockSpec((tg, TOP_K), lambda i, pos: (i, 0)),
                      pl.BlockSpec(memory_space=pl.ANY), pl.BlockSpec((1, d), lambda i, pos: (0, 0))],
            out_specs=pl.BlockSpec((tg, d), lambda i, pos: (i, 0)),
            scratch_shapes=[pltpu.VMEM((2, TOP_K, tg, d), F32), pltpu.SemaphoreType.DMA((2,))],
        ),
        out_shape=jax.ShapeDtypeStruct((t, d), F32),
        compiler_params=_params(("arbitrary",), nb),
        name="combine_norm",
    )(pos, h, wts, ys, g.reshape(1, d))


def moe_ffn_then_norm(h, norm_g, router, w1, w3, w2, final_g, *, tm=512):
    t = h.shape[0]
    tm = _pick(t * TOP_K, tm)
    hn, idx, wts = rmsnorm_router(h, norm_g, router)
    pos, src_tok, sched = _route(idx, tm)
    xs = gather_rows(hn, src_tok, sched[-1] * tm, out_dtype=BF16)
    acts = moe_ffn(xs, sched, (w1, w3), out_dtype=BF16, tm=tm, name="moe_up")
    ys = moe_ffn(acts, sched, (w2,), out_dtype=F32, tm=tm, tn=1024, name="moe_down")
    return combine_norm(h, wts, pos, ys, final_g)


def _mixer(h, norm1, w_in, w_gate, conv_w, conv_b, conv_ln_g, conv_ln_b, pool_w, pool_scale, gmlp_ln_g, gmlp_ln_b,
           gmlp_ws, gmlp_b, w_branch, w_out, tables, *, batch, seq):
    bw = w_branch.shape[1]
    xn = rmsnorm(h, norm1, BF16)
    hin = matmul(xn, w_in, out_dtype=BF16)
    ya = conv_branch(hin, conv_w, conv_b, conv_ln_g, conv_ln_b, seq=seq, bw=bw)
    yb = pool_branch(hin, pool_w, pool_scale, seq=seq, bw=bw, col=2)
    yc = fourier_branch(hin, *tables, batch=batch, seq=seq, bw=bw, col=3)
    yd = gmlp_branch(hin, gmlp_ln_g, gmlp_ln_b, gmlp_ws, gmlp_b, bw=bw, col_u=4, col_v=5)
    gates = matmul(xn, w_gate, out_dtype=BF16, epilogue="sigmoid")
    merged = merge_branches((ya, yb, yc, yd), gates, w_branch)
    return matmul(merged, w_out, out_dtype=F32, res=h, tm=512)


def kernel(x, l0_norm1, l0_w_in, l0_w_gate, l0_conv_w, l0_conv_b, l0_conv_ln_g, l0_conv_ln_b, l0_pool_w, l0_pool_scale, l0_gmlp_ln_g, l0_gmlp_ln_b, l0_gmlp_ws, l0_gmlp_b, l0_w_branch, l0_w_out, l0_norm2, l0_ffn_w1, l0_ffn_w3, l0_ffn_w2, l1_norm1, l1_w_in, l1_w_gate, l1_conv_w, l1_conv_b, l1_conv_ln_g, l1_conv_ln_b, l1_pool_w, l1_pool_scale, l1_gmlp_ln_g, l1_gmlp_ln_b, l1_gmlp_ws, l1_gmlp_b, l1_w_branch, l1_w_out, l1_norm2, l1_router, l1_exp_w1, l1_exp_w3, l1_exp_w2, final_norm):
    batch, seq, d = x.shape
    bw = l0_w_branch.shape[1]
    tables = _dft_tables(seq, bw // N_FFT_GROUPS)
    h = x.reshape(batch * seq, d)

    h = _mixer(h, l0_norm1, l0_w_in, l0_w_gate, l0_conv_w, l0_conv_b, l0_conv_ln_g, l0_conv_ln_b, l0_pool_w,
               l0_pool_scale, l0_gmlp_ln_g, l0_gmlp_ln_b, l0_gmlp_ws, l0_gmlp_b, l0_w_branch, l0_w_out, tables,
               batch=batch, seq=seq)
    hn = rmsnorm(h, l0_norm2, BF16)
    acts = swiglu_up(hn, l0_ffn_w1, l0_ffn_w3)
    half = l0_ffn_w2.shape[0] // 2
    h = matmul(acts, l0_ffn_w2, out_dtype=F32, res=h, tn=512, k_block=(0, half))
    h = matmul(acts, l0_ffn_w2, out_dtype=F32, res=h, tn=512, k_block=(1, half))

    h = _mixer(h, l1_norm1, l1_w_in, l1_w_gate, l1_conv_w, l1_conv_b, l1_conv_ln_g, l1_conv_ln_b, l1_pool_w,
               l1_pool_scale, l1_gmlp_ln_g, l1_gmlp_ln_b, l1_gmlp_ws, l1_gmlp_b, l1_w_branch, l1_w_out, tables,
               batch=batch, seq=seq)
    out = moe_ffn_then_norm(h, l1_norm2, l1_router, l1_exp_w1, l1_exp_w3, l1_exp_w2, final_norm)
    return out.reshape(batch, seq, d)
```

```python
import functools

import numpy as np
import jax
import jax.numpy as jnp
from jax import lax
from jax.experimental import pallas as pl
from jax.experimental.pallas import tpu as pltpu

F32 = jnp.float32
BF16 = jnp.bfloat16

N_BRANCH = 4
CONV_WIDTH = 31
CONV_PAD = CONV_WIDTH // 2
POOL_WINDOWS = (2, 4, 8, 16)
N_FFT_GROUPS = 4
GMLP_CHUNK = 128
GMLP_HEAD_DIM = 128
N_EXPERTS = 8
TOP_K = 2
EPS = 1e-6

HALO = 16
V7X_VMEM_BYTES = 64 * 1024 * 1024
VMEM_CAP = V7X_VMEM_BYTES - 6 * 1024 * 1024


def _vmem_limit(nbytes):
    return int(min(VMEM_CAP, nbytes + nbytes // 4 + (4 << 20)))


def _nbytes(shape, dtype):
    return int(np.prod(shape)) * jnp.dtype(dtype).itemsize


def _params(sem, nbytes):
    return pltpu.CompilerParams(dimension_semantics=sem, vmem_limit_bytes=_vmem_limit(nbytes))


def _pick(n, pref):
    t = min(n, pref)
    while n % t:
        t -= 8
    return t


def _rmsnorm_rows(x, g):
    return x * lax.rsqrt(jnp.mean(x * x, axis=-1, keepdims=True) + EPS) * g


def _rmsnorm_kernel(x_ref, g_ref, o_ref):
    o_ref[...] = _rmsnorm_rows(x_ref[...], g_ref[...]).astype(o_ref.dtype)


def rmsnorm(x, g, out_dtype):
    t, d = x.shape
    tm = _pick(t, 256)
    nb = 2 * _nbytes((tm, d), F32) + 2 * _nbytes((tm, d), out_dtype)
    return pl.pallas_call(
        _rmsnorm_kernel,
        grid=(t // tm,),
        in_specs=[pl.BlockSpec((tm, d), lambda i: (i, 0)), pl.BlockSpec((1, d), lambda i: (0, 0))],
        out_specs=pl.BlockSpec((tm, d), lambda i: (i, 0)),
        out_shape=jax.ShapeDtypeStruct((t, d), out_dtype),
        compiler_params=_params(("parallel",), nb),
        name="rmsnorm",
    )(x, g.reshape(1, d))


def _rmsnorm_router_kernel(x_ref, g_ref, r_ref, o_ref, idx_ref, wts_ref):
    y = _rmsnorm_rows(x_ref[...], g_ref[...])
    o_ref[...] = y
    n_e = r_ref.shape[1] // 2
    y_hi = y.astype(BF16)
    y_lo = (y - y_hi.astype(F32)).astype(BF16)
    p_hi = jnp.dot(y_hi, r_ref[...], preferred_element_type=F32)
    p_lo = jnp.dot(y_lo, r_ref[...], preferred_element_type=F32)
    logits = p_hi[:, :n_e] + (p_hi[:, n_e:] + p_lo[:, :n_e])
    lane = lax.broadcasted_iota(jnp.int32, logits.shape, 1)
    m1 = jnp.max(logits, axis=-1, keepdims=True)
    i1 = jnp.min(jnp.where(logits == m1, lane, n_e), axis=-1, keepdims=True)
    rest = jnp.where(lane == i1, -jnp.inf, logits)
    m2 = jnp.max(rest, axis=-1, keepdims=True)
    i2 = jnp.min(jnp.where(rest == m2, lane, n_e), axis=-1, keepdims=True)
    e2 = jnp.exp(m2 - m1)
    den = 1.0 + e2
    slot = lax.broadcasted_iota(jnp.int32, idx_ref.shape, 1)
    idx_ref[...] = jnp.where(slot == 0, i1, i2)
    wts_ref[...] = jnp.where(slot == 0, 1.0 / den, e2 / den)


def rmsnorm_router(x, g, router):
    t, d = x.shape
    n_e = router.shape[1]
    tm = _pick(t, 256)
    r_hi = router.astype(BF16)
    r_lo = (router - r_hi.astype(F32)).astype(BF16)
    r_parts = jnp.concatenate([r_hi, r_lo], axis=1)
    nb = 6 * _nbytes((tm, d), F32) + 2 * _nbytes((d, 128), BF16)
    return pl.pallas_call(
        _rmsnorm_router_kernel,
        grid=(t // tm,),
        in_specs=[pl.BlockSpec((tm, d), lambda i: (i, 0)), pl.BlockSpec((1, d), lambda i: (0, 0)),
                  pl.BlockSpec((d, 2 * n_e), lambda i: (0, 0))],
        out_specs=[pl.BlockSpec((tm, d), lambda i: (i, 0)), pl.BlockSpec((tm, TOP_K), lambda i: (i, 0)),
                   pl.BlockSpec((tm, TOP_K), lambda i: (i, 0))],
        out_shape=[jax.ShapeDtypeStruct((t, d), F32), jax.ShapeDtypeStruct((t, TOP_K), jnp.int32),
                   jax.ShapeDtypeStruct((t, TOP_K), F32)],
        compiler_params=_params(("parallel",), nb),
        name="rmsnorm_router",
    )(x, g.reshape(1, d), r_parts)


MXU_COLS = 256
CAST_ROWS = 128


def _sigmoid(x):
    return 0.5 * jnp.tanh(0.5 * x) + 0.5


def _col_chunks(tn):
    cn = MXU_COLS if tn % MXU_COLS == 0 else tn
    return [(c0, cn) for c0 in range(0, tn, cn)]


def _cast_staged(stage_refs, wbf_refs):
    k = stage_refs[0].shape[0]

    def body(c, carry):
        rows = pl.ds(pl.multiple_of(c * CAST_ROWS, CAST_ROWS), CAST_ROWS)
        for stage_ref, wbf_ref in zip(stage_refs, wbf_refs):
            wbf_ref[rows, :] = stage_ref[rows, :].astype(BF16)
        return carry

    lax.fori_loop(0, k // CAST_ROWS, body, 0)


def _matmul_kernel(*refs, has_res, epilogue, tk, tn, nj, k0):
    x_ref, w_ref = refs[0], refs[1]
    pos = 2
    r_ref = None
    if has_res:
        r_ref = refs[pos]
        pos += 1
    o_ref = refs[pos]
    stage_ref, wbf_ref, sem = refs[pos + 1:pos + 4]
    j = pl.program_id(0)

    def copy(jj):
        cols = pl.ds(pl.multiple_of(jj * tn, tn), tn)
        return pltpu.make_async_copy(w_ref.at[pl.ds(k0, tk), cols], stage_ref, sem)

    @pl.when(pl.program_id(1) == 0)
    def _():
        @pl.when(j == 0)
        def _():
            copy(j).start()

        copy(j).wait()
        _cast_staged([stage_ref], [wbf_ref])

        @pl.when(j + 1 < nj)
        def _():
            copy(j + 1).start()

    acc = jnp.dot(x_ref[...], wbf_ref[...], preferred_element_type=F32)
    if epilogue == "sigmoid":
        acc = _sigmoid(acc)
    if has_res:
        acc = acc + r_ref[...]
    o_ref[...] = acc.astype(o_ref.dtype)


def matmul(x, w, *, out_dtype, res=None, epilogue=None, tm=1024, tn=1024, k_block=None):
    m, kx = x.shape
    kw, n = w.shape
    assert w.dtype == F32
    if k_block is None:
        assert kx == kw
        kk, tk = 0, kx
    else:
        kk, tk = k_block
    tm = _pick(m, tm)
    tn = _pick(n, tn)
    nj = n // tn
    nb = 2 * _nbytes((tm, tk), BF16) + 2 * _nbytes((tm, tn), out_dtype) + 2 * _nbytes((tm, tn), F32)
    scratch = [pltpu.VMEM((tk, tn), F32), pltpu.VMEM((tk, tn), BF16), pltpu.SemaphoreType.DMA(())]
    nb += _nbytes((tk, tn), F32) + _nbytes((tk, tn), BF16)
    in_specs = [pl.BlockSpec((tm, tk), lambda j, i: (i, kk)), pl.BlockSpec(memory_space=pl.ANY)]
    args = [x, w]
    if res is not None:
        in_specs.append(pl.BlockSpec((tm, tn), lambda j, i: (i, j)))
        args.append(res)
        nb += 2 * _nbytes((tm, tn), res.dtype)
    return pl.pallas_call(
        functools.partial(_matmul_kernel, has_res=res is not None, epilogue=epilogue,
                          tk=tk, tn=tn, nj=nj, k0=kk * tk),
        grid=(nj, m // tm),
        in_specs=in_specs,
        out_specs=pl.BlockSpec((tm, tn), lambda j, i: (i, j)),
        out_shape=jax.ShapeDtypeStruct((m, n), out_dtype),
        scratch_shapes=scratch,
        compiler_params=_params(("arbitrary", "arbitrary"), nb),
        name="matmul",
    )(*args)


def swiglu_up(x, w1, w3, *, tm=2048, tn=256):
    tm = _pick(x.shape[0], tm)
    n_tiles = x.shape[0] // tm
    zeros = jnp.zeros((n_tiles,), jnp.int32)
    ones = jnp.ones((n_tiles,), jnp.int32)
    first = zeros.at[0].set(1)
    sched = (zeros, first, zeros, ones, ones, jnp.full((1,), n_tiles, jnp.int32))
    return moe_ffn(x, sched, (w1[None], w3[None]), out_dtype=BF16, tm=tm, tn=tn, name="swiglu_up")


def _merge_kernel(ya_ref, yb_ref, yc_ref, yd_ref, g0_ref, g1_ref, g2_ref, g3_ref, wb_ref, o_ref, wbf_ref):
    @pl.when(pl.program_id(1) == 0)
    def _():
        wbf_ref[...] = wb_ref[...].astype(BF16)

    pairs = ((ya_ref, g0_ref), (yb_ref, g1_ref), (yc_ref, g2_ref), (yd_ref, g3_ref))
    for c0, cn in _col_chunks(o_ref.shape[1]):
        acc = None
        for g, (y_ref, g_ref) in enumerate(pairs):
            proj = jnp.dot(y_ref[...], wbf_ref[g, :, c0:c0 + cn], preferred_element_type=F32)
            term = g_ref[:, c0:c0 + cn].astype(F32) * proj
            acc = term if acc is None else acc + term
        o_ref[:, c0:c0 + cn] = acc.astype(o_ref.dtype)


def merge_branches(branches, gates, w_branch, *, tm=1024, tn=512):
    t, bw = branches[0].shape
    d = w_branch.shape[2]
    tm = _pick(t, tm)
    tn = _pick(d, tn)
    nj = d // tn
    y_spec = pl.BlockSpec((tm, bw), lambda j, i: (i, 0))
    gate_specs = [pl.BlockSpec((tm, tn), functools.partial(lambda j, i, g: (i, g * nj + j), g=g))
                  for g in range(N_BRANCH)]
    nb = (2 * N_BRANCH * _nbytes((tm, bw), BF16) + 2 * N_BRANCH * _nbytes((tm, tn), BF16)
          + 2 * _nbytes((N_BRANCH, bw, tn), F32) + _nbytes((N_BRANCH, bw, tn), BF16)
          + 2 * _nbytes((tm, tn), BF16) + 3 * _nbytes((tm, tn), F32))
    return pl.pallas_call(
        _merge_kernel,
        grid=(nj, t // tm),
        in_specs=[y_spec] * N_BRANCH + gate_specs + [pl.BlockSpec((N_BRANCH, bw, tn), lambda j, i: (0, 0, j))],
        out_specs=pl.BlockSpec((tm, tn), lambda j, i: (i, j)),
        out_shape=jax.ShapeDtypeStruct((t, d), BF16),
        scratch_shapes=[pltpu.VMEM((N_BRANCH, bw, tn), BF16)],
        compiler_params=_params(("arbitrary", "arbitrary"), nb),
        name="merge_branches",
    )(*branches, gates, gates, gates, gates, w_branch)


def _halo_specs(ts, width, col, seq_tiles_total):
    r = ts // HALO
    last = seq_tiles_total * r - 1
    prev = pl.BlockSpec((HALO, width), lambda i: (jnp.maximum(i * r - 1, 0), col))
    cur = pl.BlockSpec((ts, width), lambda i: (i, col))
    nxt = pl.BlockSpec((HALO, width), lambda i: (jnp.minimum((i + 1) * r, last), col))
    return [prev, cur, nxt]


def _edge_flags(ts, seq):
    tiles_per_seq = seq // ts
    k = pl.program_id(0) % tiles_per_seq
    return k == 0, k == tiles_per_seq - 1, k * ts


SUBLANES = 8


def _conv_kernel(lp_ref, lc_ref, ln_ref, gp_ref, gc_ref, gn_ref, cw_ref, cb_ref, lg_ref, lb_ref, o_ref,
                 abuf_ref, ash_ref, ybuf_ref, *, ts, seq):
    first, last, _ = _edge_flags(ts, seq)
    width = lc_ref.shape[1]
    glu = lambda l_ref, g_ref: l_ref[...].astype(F32) * jax.nn.sigmoid(g_ref[...].astype(F32))
    abuf_ref[0:HALO, :] = jnp.where(first, 0.0, glu(lp_ref, gp_ref))
    abuf_ref[HALO + ts:2 * HALO + ts, :] = jnp.where(last, 0.0, glu(ln_ref, gn_ref))
    abuf_ref[HALO:HALO + ts, :] = glu(lc_ref, gc_ref)

    span = ash_ref.shape[1]
    for p in range(1, SUBLANES):
        ash_ref[p - 1] = abuf_ref[p:p + span, :]

    rows = 64 if ts % 64 == 0 else ts
    for r0 in range(0, ts, rows):
        for c0 in range(0, width, 128):
            acc = jnp.zeros((rows, 128), F32)
            for k in range(CONV_WIDTH):
                q, p = divmod(HALO - CONV_PAD + k, SUBLANES)
                start = r0 + q * SUBLANES
                src = abuf_ref if p == 0 else ash_ref.at[p - 1]
                acc = acc + cw_ref[k:k + 1, c0:c0 + 128] * src[start:start + rows, c0:c0 + 128]
            ybuf_ref[r0:r0 + rows, c0:c0 + 128] = acc + cb_ref[:, c0:c0 + 128]

    y = ybuf_ref[...]
    mu = jnp.mean(y, axis=-1, keepdims=True)
    yc = y - mu
    var = jnp.mean(yc * yc, axis=-1, keepdims=True)
    z = yc * lax.rsqrt(var + EPS) * lg_ref[...] + lb_ref[...]
    o_ref[...] = jax.nn.silu(z).astype(o_ref.dtype)


def conv_branch(hin, conv_w, conv_b, ln_g, ln_b, *, seq, bw, ts=256):
    t = hin.shape[0]
    ts = _pick(seq, ts)
    n_tiles = t // ts
    row = lambda v: v.reshape(1, bw)
    const = lambda shape: pl.BlockSpec(shape, lambda i: (0, 0))
    span = ts + 2 * HALO - SUBLANES
    nb = (4 * _nbytes((ts + 2 * HALO, bw), BF16) + 2 * _nbytes((ts, bw), BF16) + _nbytes((ts + 2 * HALO, bw), F32)
          + (SUBLANES - 1) * _nbytes((span, bw), F32) + 5 * _nbytes((ts, bw), F32))
    return pl.pallas_call(
        functools.partial(_conv_kernel, ts=ts, seq=seq),
        grid=(n_tiles,),
        in_specs=_halo_specs(ts, bw, 0, n_tiles) + _halo_specs(ts, bw, 1, n_tiles)
        + [const((CONV_WIDTH, bw)), const((1, bw)), const((1, bw)), const((1, bw))],
        out_specs=pl.BlockSpec((ts, bw), lambda i: (i, 0)),
        out_shape=jax.ShapeDtypeStruct((t, bw), BF16),
        scratch_shapes=[pltpu.VMEM((ts + 2 * HALO, bw), F32), pltpu.VMEM((SUBLANES - 1, span, bw), F32),
                        pltpu.VMEM((ts, bw), F32)],
        compiler_params=_params(("parallel",), nb),
        name="conv_branch",
    )(hin, hin, hin, hin, hin, hin, conv_w, row(conv_b), row(ln_g), row(ln_b))


def _pool_kernel(pp_ref, pc_ref, pn_ref, pw_ref, ps_ref, o_ref, pbuf_ref, *, ts, seq):
    first, last, pos0 = _edge_flags(ts, seq)
    n_g = len(POOL_WINDOWS)
    pg = pc_ref.shape[1] // n_g
    pbuf_ref[0:HALO, :] = jnp.where(first, 0.0, pp_ref[...].astype(F32))
    pbuf_ref[HALO + ts:2 * HALO + ts, :] = jnp.where(last, 0.0, pn_ref[...].astype(F32))
    pbuf_ref[HALO:HALO + ts, :] = pc_ref[...].astype(F32)
    pos = pos0 + lax.broadcasted_iota(jnp.int32, (ts, 1), 0)
    for g, w in enumerate(POOL_WINDOWS):
        cols = slice(g * pg, (g + 1) * pg)
        win = None
        for d in range(-(w // 2), w // 2):
            term = pbuf_ref[HALO + d:HALO + d + ts, cols]
            win = term if win is None else win + term
        lo = jnp.maximum(pos - w // 2, 0)
        hi = jnp.minimum(pos + w // 2 - 1, seq - 1)
        cnt = (hi - lo + 1).astype(F32)
        pooled = win / cnt - pbuf_ref[HALO:HALO + ts, cols]
        mixed = jnp.dot(pooled.astype(BF16), pw_ref[g].astype(BF16), preferred_element_type=F32)
        o_ref[:, cols] = (mixed * ps_ref[:, cols]).astype(o_ref.dtype)


def pool_branch(hin, pool_w, pool_scale, *, seq, bw, col, ts=256):
    t = hin.shape[0]
    ts = _pick(seq, ts)
    n_tiles = t // ts
    nb = 2 * _nbytes((ts + 2 * HALO, bw), F32) + 2 * _nbytes((ts, bw), BF16) + 4 * _nbytes((ts + 2 * HALO, bw), F32)
    nb += 2 * _nbytes(pool_w.shape, F32)
    return pl.pallas_call(
        functools.partial(_pool_kernel, ts=ts, seq=seq),
        grid=(n_tiles,),
        in_specs=_halo_specs(ts, bw, col, n_tiles)
        + [pl.BlockSpec(pool_w.shape, lambda i: (0, 0, 0)), pl.BlockSpec((1, bw), lambda i: (0, 0))],
        out_specs=pl.BlockSpec((ts, bw), lambda i: (i, 0)),
        out_shape=jax.ShapeDtypeStruct((t, bw), BF16),
        scratch_shapes=[pltpu.VMEM((ts + 2 * HALO, bw), F32)],
        compiler_params=_params(("parallel",), nb),
        name="pool_branch",
    )(hin, hin, hin, pool_w, pool_scale.reshape(1, bw))


def _dft_tables(seq, group):
    c = np.arange(group)
    ang = 2.0 * np.pi * np.outer(c, c) / group
    scale = 1.0 / np.sqrt(float(seq) * group)
    chan = np.concatenate([np.cos(ang), np.sin(ang)], axis=1) * scale
    half = seq // 2
    fa = 64 if half % 64 == 0 else 1
    sp = np.arange(seq)[:, None]
    ang_a = 2.0 * np.pi * ((sp * fa * np.arange(half // fa)[None, :]) % seq) / seq
    ang_b = 2.0 * np.pi * ((sp * np.arange(fa)[None, :]) % seq) / seq
    ca, sa = jnp.asarray(np.cos(ang_a), F32)[:, :, None], jnp.asarray(np.sin(ang_a), F32)[:, :, None]
    cb, sb = jnp.asarray(np.cos(ang_b), F32)[:, None, :], jnp.asarray(np.sin(ang_b), F32)[:, None, :]
    cos_s = (ca * cb - sa * sb).reshape(seq, half)
    sin_s = (sa * cb + ca * sb).reshape(seq, half)
    pos = jnp.concatenate([cos_s, -sin_s], axis=1).astype(BF16)
    return jnp.asarray(chan, F32).astype(BF16), pos


FOLD_EDGE_ROWS = 16


def _fold_chan_dft_kernel(fc_ref, fm_ref, fx_ref, flip_ref, tbl_ref, o_ref, eh_ref, *, tiles_half):
    i = pl.program_id(0) % tiles_half
    n_g = N_FFT_GROUPS
    fg = fc_ref.shape[1] // n_g
    ts = fc_ref.shape[0]
    tbl = tbl_ref[...]
    mirror = jnp.dot(flip_ref[...], fm_ref[...], preferred_element_type=F32)
    row = lax.broadcasted_iota(jnp.int32, (ts, 1), 0)
    edge = fx_ref[0:1, :].astype(F32)
    partner = jnp.where(jnp.logical_and(row == 0, i > 0), edge, mirror)
    f = fc_ref[...].astype(F32)
    even = (f + partner).astype(BF16)
    odd = (f - partner).astype(BF16)
    for g in range(n_g):
        cols = slice(g * fg, (g + 1) * fg)
        o_ref[0, :, cols] = jnp.dot(even[:, cols], tbl[:, :fg], preferred_element_type=F32).astype(o_ref.dtype)
        o_ref[1, :, cols] = jnp.dot(odd[:, cols], tbl[:, fg:], preferred_element_type=F32).astype(o_ref.dtype)

    @pl.when(i == 0)
    def _():
        edge_row = lax.broadcasted_iota(jnp.int32, (FOLD_EDGE_ROWS, 1), 0)
        mid = jnp.where(edge_row == 0, fx_ref[...].astype(F32), 0.0).astype(BF16)
        for g in range(n_g):
            cols = slice(g * fg, (g + 1) * fg)
            eh_ref[:, cols] = jnp.dot(mid[:, cols], tbl[:, :fg], preferred_element_type=F32).astype(eh_ref.dtype)


def _fourier_pos_kernel(t_ref, r_ref, eh_ref, o_ref):
    acc = jnp.dot(t_ref[...], r_ref[...], preferred_element_type=F32)
    row = lax.broadcasted_iota(jnp.int32, (t_ref.shape[0], 1), 0)
    sign = jnp.where(jnp.bitwise_and(row, 1) == 0, 1.0, -1.0)
    o_ref[...] = (acc + sign * eh_ref[0:1, :].astype(F32)).astype(o_ref.dtype)


def fourier_branch(hin, chan_tbl, pos_tbl, *, batch, seq, bw, col, ts=512):
    t = hin.shape[0]
    half = seq // 2
    ts = _pick(half, ts)
    assert ts % FOLD_EDGE_ROWS == 0
    tiles = seq // ts
    th = tiles // 2
    edge_blocks = seq // FOLD_EDGE_ROWS
    flip = np.zeros((ts, ts), np.float32)
    flip[np.arange(1, ts), ts - np.arange(1, ts)] = 1.0
    flip = jnp.asarray(flip, BF16)

    def edge_block(s):
        i = s % th
        pos = jnp.where(i == 0, half, seq - i * ts)
        return (s // th) * edge_blocks + pos // FOLD_EDGE_ROWS, col

    nb = 6 * _nbytes((ts, bw), BF16) + 2 * _nbytes(chan_tbl.shape, BF16) + 2 * _nbytes((ts, ts), BF16)
    nb += 6 * _nbytes((ts, bw), F32)
    proj, mid = pl.pallas_call(
        functools.partial(_fold_chan_dft_kernel, tiles_half=th),
        grid=(batch * th,),
        in_specs=[pl.BlockSpec((ts, bw), lambda s: ((s // th) * tiles + s % th, col)),
                  pl.BlockSpec((ts, bw), lambda s: ((s // th) * tiles + tiles - 1 - s % th, col)),
                  pl.BlockSpec((FOLD_EDGE_ROWS, bw), edge_block),
                  pl.BlockSpec((ts, ts), lambda s: (0, 0)), pl.BlockSpec(chan_tbl.shape, lambda s: (0, 0))],
        out_specs=[pl.BlockSpec((None, 2, ts, bw), lambda s: (s // th, 0, s % th, 0)),
                   pl.BlockSpec((None, FOLD_EDGE_ROWS, bw), lambda s: (s // th, 0, 0))],
        out_shape=[jax.ShapeDtypeStruct((batch, 2, half, bw), BF16),
                   jax.ShapeDtypeStruct((batch, FOLD_EDGE_ROWS, bw), BF16)],
        compiler_params=_params(("arbitrary",), nb),
        name="fourier_channels",
    )(hin, hin, hin, flip, chan_tbl)
    proj = proj.reshape(batch, seq, bw)
    tm = _pick(seq, 1024)
    tn = _pick(bw, 512)
    ni = seq // tm
    nb = 2 * _nbytes((tm, seq), BF16) + 2 * _nbytes((seq, tn), BF16) + 2 * _nbytes((tm, tn), BF16)
    nb += 2 * _nbytes((tm, tn), F32)
    return pl.pallas_call(
        _fourier_pos_kernel,
        grid=(batch, bw // tn, ni),
        in_specs=[pl.BlockSpec((tm, seq), lambda b, j, i: (i, 0)),
                  pl.BlockSpec((None, seq, tn), lambda b, j, i: (b, 0, j)),
                  pl.BlockSpec((None, FOLD_EDGE_ROWS, tn), lambda b, j, i: (b, 0, j))],
        out_specs=pl.BlockSpec((tm, tn), lambda b, j, i: (b * ni + i, j)),
        out_shape=jax.ShapeDtypeStruct((t, bw), BF16),
        compiler_params=_params(("parallel", "arbitrary", "arbitrary"), nb),
        name="fourier_positions",
    )(pos_tbl, proj, mid)


def _gmlp_kernel(u_ref, v_ref, lg_ref, lb_ref, ws_ref, bias_ref, o_ref, *, ts):
    v = v_ref[...].astype(F32)
    mu = jnp.mean(v, axis=-1, keepdims=True)
    vc = v - mu
    var = jnp.mean(vc * vc, axis=-1, keepdims=True)
    vn = (vc * lax.rsqrt(var + EPS) * lg_ref[...] + lb_ref[...]).astype(BF16)
    n_chunks = ts // GMLP_CHUNK
    n_heads = v.shape[1] // GMLP_HEAD_DIM
    for h in range(n_heads):
        cols = slice(h * GMLP_HEAD_DIM, (h + 1) * GMLP_HEAD_DIM)
        rhs = jnp.concatenate([vn[c * GMLP_CHUNK:(c + 1) * GMLP_CHUNK, cols] for c in range(n_chunks)], axis=1)
        s = jnp.dot(ws_ref[h].astype(BF16), rhs, preferred_element_type=F32)
        for c in range(n_chunks):
            rows = slice(c * GMLP_CHUNK, (c + 1) * GMLP_CHUNK)
            sc = s[:, c * GMLP_HEAD_DIM:(c + 1) * GMLP_HEAD_DIM] + bias_ref[:, cols]
            o_ref[rows, cols] = (u_ref[rows, cols].astype(F32) * sc).astype(o_ref.dtype)


def gmlp_branch(hin, ln_g, ln_b, ws, bias, *, bw, col_u, col_v, ts=512):
    t = hin.shape[0]
    ts = _pick(t, ts)
    assert ts % GMLP_CHUNK == 0
    n_heads = bw // GMLP_HEAD_DIM
    bias_rows = jnp.repeat(bias.T, GMLP_HEAD_DIM, axis=1)
    nb = 4 * _nbytes((ts, bw), F32) + 2 * _nbytes((ts, bw), BF16) + 6 * _nbytes((ts, bw), F32)
    return pl.pallas_call(
        functools.partial(_gmlp_kernel, ts=ts),
        grid=(t // ts,),
        in_specs=[pl.BlockSpec((ts, bw), lambda i: (i, col_u)), pl.BlockSpec((ts, bw), lambda i: (i, col_v)),
                  pl.BlockSpec((1, bw), lambda i: (0, 0)), pl.BlockSpec((1, bw), lambda i: (0, 0)),
                  pl.BlockSpec((n_heads, GMLP_CHUNK, GMLP_CHUNK), lambda i: (0, 0, 0)),
                  pl.BlockSpec((GMLP_CHUNK, bw), lambda i: (0, 0))],
        out_specs=pl.BlockSpec((ts, bw), lambda i: (i, 0)),
        out_shape=jax.ShapeDtypeStruct((t, bw), BF16),
        compiler_params=_params(("parallel",), nb),
        name="gmlp_branch",
    )(hin, hin, ln_g.reshape(1, bw), ln_b.reshape(1, bw), ws, bias_rows)


ROUTE_BLOCK = 128


def _route(idx, tm):
    a = idx.size
    n_tiles = a // tm + N_EXPERTS
    experts = jnp.arange(N_EXPERTS, dtype=jnp.int32)
    e_flat = idx.reshape(a)
    onehot = (e_flat[:, None] == experts[None, :]).astype(jnp.int32)
    blk = ROUTE_BLOCK if a % ROUTE_BLOCK == 0 else a
    oh_blocks = onehot.astype(F32).reshape(a // blk, blk, N_EXPERTS)
    within = jnp.einsum("ij,bjk->bik", jnp.tril(jnp.ones((blk, blk), F32)), oh_blocks)
    totals = within[:, -1, :]
    offsets = jnp.cumsum(totals, axis=0) - totals
    csum = (within + offsets[:, None, :]).reshape(a, N_EXPERTS).astype(jnp.int32)
    counts = csum[-1]
    rank = jnp.sum((csum - onehot) * onehot, axis=1)
    tiles_e = (counts + tm - 1) // tm
    tile_end = jnp.cumsum(tiles_e)
    tile_start = tile_end - tiles_e
    pos = jnp.sum(onehot * tile_start[None, :], axis=1) * tm + rank
    src_tok = jnp.zeros((n_tiles * tm,), jnp.int32).at[pos].set(jnp.arange(a, dtype=jnp.int32) // TOP_K)
    n_used = tile_end[-1]
    tile_ids = jnp.minimum(jnp.arange(n_tiles, dtype=jnp.int32), n_used - 1)
    tile_expert = jnp.sum((tile_ids[:, None] >= tile_end[None, :]).astype(jnp.int32), axis=1)
    first = jnp.concatenate([jnp.ones((1,), jnp.int32), (tile_expert[1:] != tile_expert[:-1]).astype(jnp.int32)])
    present = tiles_e > 0
    first_present = jnp.min(jnp.where(present, experts, N_EXPERTS))
    last_present = jnp.max(jnp.where(present, experts, -1))
    later = jnp.logical_and(present[None, :], experts[None, :] > experts[:, None])
    next_present = jnp.min(jnp.where(later, experts[None, :], N_EXPERTS), axis=1)
    next_present = jnp.where(next_present == N_EXPERTS, first_present, next_present)
    nxt = next_present[tile_expert]
    last_group = (tile_expert == last_present).astype(jnp.int32)
    rows_valid = counts[tile_expert] - (tile_ids - tile_start[tile_expert]) * tm
    full = (rows_valid > tm // 2).astype(jnp.int32)
    sched = (tile_expert, first, nxt, last_group, full, n_used.reshape(1).astype(jnp.int32))
    return pos, src_tok, sched


def _row_copy(src_hbm, row, dst_ref, r, sem):
    return pltpu.make_async_copy(src_hbm.at[pl.ds(row, 1), :], dst_ref.at[pl.ds(r, 1), :], sem)


ROW_DMA_UNROLL = 8
DMA_PRIORITIES = 2


def _gather_rows_kernel(tok_ref, nrows_ref, src_hbm, o_ref, buf_ref, sem, *, tg, n_steps):
    i = pl.program_id(0)

    def start_step(step):
        slot = step % 2

        def issue(pair, c):
            for prio in range(DMA_PRIORITIES):
                r = pair * DMA_PRIORITIES + prio
                _row_copy(src_hbm, tok_ref[step * tg + r], buf_ref.at[slot], r, sem.at[slot]).start(priority=prio)
            return c

        lax.fori_loop(0, tg // DMA_PRIORITIES, issue, 0, unroll=ROW_DMA_UNROLL // DMA_PRIORITIES)

    def active(step):
        return step * tg < nrows_ref[0]

    @pl.when(i == 0)
    def _():
        start_step(i)

    @pl.when(jnp.logical_and(i + 1 < n_steps, active(i + 1)))
    def _():
        start_step(i + 1)

    @pl.when(active(i))
    def _():
        slot = i % 2

        def drain(r, c):
            _row_copy(src_hbm, 0, buf_ref.at[slot], r, sem.at[slot]).wait()
            return c

        lax.fori_loop(0, tg, drain, 0, unroll=ROW_DMA_UNROLL)
        o_ref[...] = buf_ref[slot].astype(o_ref.dtype)

    @pl.when(jnp.logical_not(active(i)))
    def _():
        o_ref[...] = jnp.zeros_like(o_ref)


def gather_rows(src, tok, n_rows, *, out_dtype, tg=256):
    p = tok.shape[0]
    d = src.shape[1]
    tg = _pick(p, tg)
    n_steps = p // tg
    nb = 2 * _nbytes((tg, d), F32) + 2 * _nbytes((tg, d), out_dtype) + _nbytes((tg, d), F32)
    return pl.pallas_call(
        functools.partial(_gather_rows_kernel, tg=tg, n_steps=n_steps),
        grid_spec=pltpu.PrefetchScalarGridSpec(
            num_scalar_prefetch=2,
            grid=(n_steps,),
            in_specs=[pl.BlockSpec(memory_space=pl.ANY)],
            out_specs=pl.BlockSpec((tg, d), lambda i, tok, nrows: (i, 0)),
            scratch_shapes=[pltpu.VMEM((2, tg, d), F32), pltpu.SemaphoreType.DMA((2,))],
        ),
        out_shape=jax.ShapeDtypeStruct((p, d), out_dtype),
        compiler_params=_params(("arbitrary",), nb),
        name="gather_rows",
    )(tok, n_rows, src)


def _moe_ffn_kernel(te_ref, first_ref, nxt_ref, lastg_ref, full_ref, nu_ref, x_ref, *refs, n_w, tn, nj, tm):
    w_hbm = refs[:n_w]
    o_ref = refs[n_w]
    stage = refs[n_w + 1:2 * n_w + 1]
    wbf = refs[2 * n_w + 1:3 * n_w + 1]
    sem = refs[3 * n_w + 1]
    j = pl.program_id(0)
    n = pl.program_id(1)
    used = n < nu_ref[0]

    def copies(e, jj):
        cols = pl.ds(pl.multiple_of(jj * tn, tn), tn)
        return [pltpu.make_async_copy(w_hbm[q].at[e, :, cols], stage[q], sem.at[q]) for q in range(n_w)]

    @pl.when(jnp.logical_and(used, first_ref[n] == 1))
    def _():
        @pl.when(jnp.logical_and(j == 0, n == 0))
        def _():
            for c in copies(te_ref[n], j):
                c.start()

        for c in copies(te_ref[n], j):
            c.wait()
        _cast_staged(stage, wbf)
        is_last = lastg_ref[n] == 1

        @pl.when(jnp.logical_not(is_last))
        def _():
            for c in copies(nxt_ref[n], j):
                c.start()

        @pl.when(jnp.logical_and(is_last, j + 1 < nj))
        def _():
            for c in copies(nxt_ref[n], j + 1):
                c.start()

    def compute(rows):
        x = x_ref[0:rows, :]
        acc = jnp.dot(x, wbf[0][...], preferred_element_type=F32)
        if n_w == 2:
            acc = acc * _sigmoid(acc) * jnp.dot(x, wbf[1][...], preferred_element_type=F32)
        o_ref[0:rows, :] = acc.astype(o_ref.dtype)

    is_full = full_ref[n] == 1

    @pl.when(jnp.logical_and(used, is_full))
    def _():
        compute(tm)

    @pl.when(jnp.logical_and(used, jnp.logical_not(is_full)))
    def _():
        compute(tm // 2)
        o_ref[tm // 2:tm, :] = jnp.zeros((tm - tm // 2, tn), o_ref.dtype)

    @pl.when(jnp.logical_not(used))
    def _():
        o_ref[...] = jnp.zeros_like(o_ref)


def moe_ffn(xs, sched, weights, *, out_dtype, tm, tn=512, name="moe_ffn"):
    p, k = xs.shape
    n_w = len(weights)
    n_out = weights[0].shape[2]
    tn = _pick(n_out, tn)
    nj = n_out // tn
    x_spec = pl.BlockSpec((tm, k), lambda j, n, te, fi, nx, lg, fu, nu: (jnp.minimum(n, nu[0] - 1), 0))
    o_spec = pl.BlockSpec((tm, tn), lambda j, n, te, fi, nx, lg, fu, nu: (n, j))
    nb = (2 * _nbytes((tm, k), BF16) + n_w * _nbytes((k, tn), F32) + n_w * _nbytes((k, tn), BF16)
          + 2 * _nbytes((tm, tn), out_dtype) + (n_w + 1) * _nbytes((tm, tn), F32))
    scratch = ([pltpu.VMEM((k, tn), F32)] * n_w + [pltpu.VMEM((k, tn), BF16)] * n_w
               + [pltpu.SemaphoreType.DMA((n_w,))])
    return pl.pallas_call(
        functools.partial(_moe_ffn_kernel, n_w=n_w, tn=tn, nj=nj, tm=tm),
        grid_spec=pltpu.PrefetchScalarGridSpec(
            num_scalar_prefetch=6,
            grid=(nj, p // tm),
            in_specs=[x_spec] + [pl.BlockSpec(memory_space=pl.ANY)] * n_w,
            out_specs=o_spec,
            scratch_shapes=scratch,
        ),
        out_shape=jax.ShapeDtypeStruct((p, n_out), out_dtype),
        compiler_params=_params(("arbitrary", "arbitrary"), nb),
        name=name,
    )(*sched, xs, *weights)


def _combine_norm_kernel(pos_ref, h_ref, wts_ref, ys_hbm, g_ref, o_ref, buf_ref, sem, *, tg, n_steps):
    i = pl.program_id(0)

    def start_step(step):
        slot = step % 2

        def issue(r, c):
            for k in range(TOP_K):
                row = pos_ref[(step * tg + r) * TOP_K + k]
                _row_copy(ys_hbm, row, buf_ref.at[slot, k], r, sem.at[slot]).start(priority=k % DMA_PRIORITIES)
            return c

        lax.fori_loop(0, tg, issue, 0, unroll=ROW_DMA_UNROLL)

    @pl.when(i == 0)
    def _():
        start_step(i)

    @pl.when(i + 1 < n_steps)
    def _():
        start_step(i + 1)

    slot = i % 2

    def drain(r, c):
        for k in range(TOP_K):
            _row_copy(ys_hbm, 0, buf_ref.at[slot, k], r, sem.at[slot]).wait()
        return c

    lax.fori_loop(0, tg, drain, 0, unroll=ROW_DMA_UNROLL)
    h = h_ref[...]
    for k in range(TOP_K):
        h = h + wts_ref[:, k:k + 1] * buf_ref[slot, k]
    o_ref[...] = _rmsnorm_rows(h, g_ref[...]).astype(o_ref.dtype)


def combine_norm(h, wts, pos, ys, g, *, tg=128):
    t, d = h.shape
    tg = _pick(t, tg)
    n_steps = t // tg
    nb = 4 * _nbytes((tg, d), F32) + 2 * TOP_K * _nbytes((tg, d), F32) + 4 * _nbytes((tg, d), F32)
    return pl.pallas_call(
        functools.partial(_combine_norm_kernel, tg=tg, n_steps=n_steps),
        grid_spec=pltpu.PrefetchScalarGridSpec(
            num_scalar_prefetch=1,
            grid=(n_steps,),
            in_specs=[pl.BlockSpec((tg, d), lambda i, pos: (i, 0)), pl.BlockSpec((tg, TOP_K), lambda i, pos: (i, 0)),
                      pl.BlockSpec(memory_space=pl.ANY), pl.BlockSpec((1, d), lambda i, pos: (0, 0))],
            out_specs=pl.BlockSpec((tg, d), lambda i, pos: (i, 0)),
            scratch_shapes=[pltpu.VMEM((2, TOP_K, tg, d), F32), pltpu.SemaphoreType.DMA((2,))],
        ),
        out_shape=jax.ShapeDtypeStruct((t, d), F32),
        compiler_params=_params(("arbitrary",), nb),
        name="combine_norm",
    )(pos, h, wts, ys, g.reshape(1, d))


def moe_ffn_then_norm(h, norm_g, router, w1, w3, w2, final_g, *, tm=512):
    t = h.shape[0]
    tm = _pick(t * TOP_K, tm)
    hn, idx, wts = rmsnorm_router(h, norm_g, router)
    pos, src_tok, sched = _route(idx, tm)
    xs = gather_rows(hn, src_tok, sched[-1] * tm, out_dtype=BF16)
    acts = moe_ffn(xs, sched, (w1, w3), out_dtype=BF16, tm=tm, name="moe_up")
    ys = moe_ffn(acts, sched, (w2,), out_dtype=F32, tm=tm, tn=1024, name="moe_down")
    return combine_norm(h, wts, pos, ys, final_g)


def _mixer(h, norm1, w_in, w_gate, conv_w, conv_b, conv_ln_g, conv_ln_b, pool_w, pool_scale, gmlp_ln_g, gmlp_ln_b,
           gmlp_ws, gmlp_b, w_branch, w_out, tables, *, batch, seq):
    bw = w_branch.shape[1]
    xn = rmsnorm(h, norm1, BF16)
    hin = matmul(xn, w_in, out_dtype=BF16)
    ya = conv_branch(hin, conv_w, conv_b, conv_ln_g, conv_ln_b, seq=seq, bw=bw)
    yb = pool_branch(hin, pool_w, pool_scale, seq=seq, bw=bw, col=2)
    yc = fourier_branch(hin, *tables, batch=batch, seq=seq, bw=bw, col=3)
    yd = gmlp_branch(hin, gmlp_ln_g, gmlp_ln_b, gmlp_ws, gmlp_b, bw=bw, col_u=4, col_v=5)
    gates = matmul(xn, w_gate, out_dtype=BF16, epilogue="sigmoid")
    merged = merge_branches((ya, yb, yc, yd), gates, w_branch)
    return matmul(merged, w_out, out_dtype=F32, res=h, tm=512)


def kernel(x, l0_norm1, l0_w_in, l0_w_gate, l0_conv_w, l0_conv_b, l0_conv_ln_g, l0_conv_ln_b, l0_pool_w, l0_pool_scale, l0_gmlp_ln_g, l0_gmlp_ln_b, l0_gmlp_ws, l0_gmlp_b, l0_w_branch, l0_w_out, l0_norm2, l0_ffn_w1, l0_ffn_w3, l0_ffn_w2, l1_norm1, l1_w_in, l1_w_gate, l1_conv_w, l1_conv_b, l1_conv_ln_g, l1_conv_ln_b, l1_pool_w, l1_pool_scale, l1_gmlp_ln_g, l1_gmlp_ln_b, l1_gmlp_ws, l1_gmlp_b, l1_w_branch, l1_w_out, l1_norm2, l1_router, l1_exp_w1, l1_exp_w3, l1_exp_w2, final_norm):
    batch, seq, d = x.shape
    bw = l0_w_branch.shape[1]
    tables = _dft_tables(seq, bw // N_FFT_GROUPS)
    h = x.reshape(batch * seq, d)

    h = _mixer(h, l0_norm1, l0_w_in, l0_w_gate, l0_conv_w, l0_conv_b, l0_conv_ln_g, l0_conv_ln_b, l0_pool_w,
               l0_pool_scale, l0_gmlp_ln_g, l0_gmlp_ln_b, l0_gmlp_ws, l0_gmlp_b, l0_w_branch, l0_w_out, tables,
               batch=batch, seq=seq)
    hn = rmsnorm(h, l0_norm2, BF16)
    acts = swiglu_up(hn, l0_ffn_w1, l0_ffn_w3)
    half = l0_ffn_w2.shape[0] // 2
    h = matmul(acts, l0_ffn_w2, out_dtype=F32, res=h, tn=512, k_block=(0, half))
    h = matmul(acts, l0_ffn_w2, out_dtype=F32, res=h, tn=512, k_block=(1, half))

    h = _mixer(h, l1_norm1, l1_w_in, l1_w_gate, l1_conv_w, l1_conv_b, l1_conv_ln_g, l1_conv_ln_b, l1_pool_w,
               l1_pool_scale, l1_gmlp_ln_g, l1_gmlp_ln_b, l1_gmlp_ws, l1_gmlp_b, l1_w_branch, l1_w_out, tables,
               batch=batch, seq=seq)
    out = moe_ffn_then_norm(h, l1_norm2, l1_router, l1_exp_w1, l1_exp_w3, l1_exp_w2, final_norm)
    return out.reshape(batch, seq, d)
```
